```python
import jax, jax.numpy as jnp
from jax import lax
import numpy as np

D_MODEL = 1024
BATCH = 4
SEQ = 8192
DEPTH = 2

CHUNK = 64
N_BRANCH = 3
CONV_WIDTH = 512
CONV_K = 31
ATT_HEADS = 8
ATT_HEAD_DIM = 64
ATT_WIDTH = ATT_HEADS * ATT_HEAD_DIM
Q_BLOCK = 128
POOL_WINDOWS = (2, 4, 8, 16)
POOL_GROUPS = len(POOL_WINDOWS)
POOL_GROUP_DIM = 128
POOL_WIDTH = POOL_GROUPS * POOL_GROUP_DIM
D_FF = 4 * D_MODEL
EPS = 1e-6
IN_SPLITS = (ATT_WIDTH, ATT_WIDTH, ATT_WIDTH, 2 * CONV_WIDTH, POOL_WIDTH, N_BRANCH * D_MODEL)
IN_WIDTH = sum(IN_SPLITS)

kernel_name = "hybrid_conv_stickbreak_pool_encoder"


def rms_norm(x, g):
    xf = x.astype(jnp.float32)
    y = xf * lax.rsqrt(jnp.mean(xf * xf, axis=-1, keepdims=True) + EPS)
    return (y * g.astype(jnp.float32)).astype(x.dtype)


def layer_norm(x, g, b):
    xf = x.astype(jnp.float32)
    mu = jnp.mean(xf, axis=-1, keepdims=True)
    xc = xf - mu
    y = xc * lax.rsqrt(jnp.mean(xc * xc, axis=-1, keepdims=True) + EPS)
    return (y * g.astype(jnp.float32) + b.astype(jnp.float32)).astype(x.dtype)


def conv_module(u, dw, dw_b, ln_g, ln_b, w_out):
    a, gate = jnp.split(u, 2, axis=-1)
    h = a * jax.nn.sigmoid(gate)
    h = lax.conv_general_dilated(
        h, dw[:, None, :].astype(h.dtype), window_strides=(1,), padding=((CONV_K - 1, 0),),
        dimension_numbers=("NWC", "WIO", "NWC"), feature_group_count=CONV_WIDTH) + dw_b
    h = jax.nn.silu(layer_norm(h, ln_g, ln_b))
    return h @ w_out


def stick_breaking_attention(q, k, v):
    S = q.shape[2]
    scale = ATT_HEAD_DIM ** -0.5
    outs = []
    for blk in range(S // Q_BLOCK):
        q0 = blk * Q_BLOCK
        kend = q0 + Q_BLOCK
        qb = q[:, :, q0:kend]
        kb = k[:, :, :kend]
        vb = v[:, :, :kend]
        z = jnp.einsum("bhqd,bhkd->bhqk", qb, kb).astype(jnp.float32) * scale
        qpos = q0 + jnp.arange(Q_BLOCK)[:, None]
        kpos = jnp.arange(kend)[None, :]
        mask = kpos < qpos
        log_beta = jax.nn.log_sigmoid(z)
        log_1m = jnp.where(mask, jax.nn.log_sigmoid(-z), 0.0)
        rev = lax.cumsum(log_1m, axis=3, reverse=True)
        a = jnp.where(mask, jnp.exp(log_beta + rev - log_1m), 0.0)
        outs.append(jnp.einsum("bhqk,bhkd->bhqd", a.astype(vb.dtype), vb))
    return jnp.concatenate(outs, axis=2)


def multiscale_pool(p, pool_w, pool_scale, w_out):
    S = p.shape[1]
    pf = p.astype(jnp.float32)
    groups = []
    for g, w in enumerate(POOL_WINDOWS):
        xg = pf[..., g * POOL_GROUP_DIM:(g + 1) * POOL_GROUP_DIM]
        c = jnp.cumsum(xg, axis=1)
        c_shift = jnp.pad(c, ((0, 0), (w, 0), (0, 0)))[:, :S]
        count = jnp.minimum(jnp.arange(S) + 1, w).astype(jnp.float32)[None, :, None]
        groups.append((c - c_shift) / count - xg)
    y = jnp.stack(groups, axis=2).astype(p.dtype)
    y = jnp.einsum("bsgc,gcd->bsgd", y, pool_w)
    y = y.reshape(y.shape[0], S, POOL_WIDTH) * pool_scale
    return y @ w_out


def setup_inputs(seed: int = 0) -> dict:
    key = jax.random.key(seed)
    ks = jax.random.split(key, 20)
    f32 = jnp.float32

    def nrm(k, shape, fan_in):
        return jax.random.normal(k, shape, f32) * (fan_in ** -0.5)

    def gain(k, shape):
        return 1.0 + 0.05 * jax.random.normal(k, shape, f32)

    return {
        "x": jax.random.normal(ks[0], (BATCH, SEQ, D_MODEL), f32),
        "mix_norm_g": gain(ks[1], (DEPTH, D_MODEL)),
        "w_in": nrm(ks[2], (DEPTH, D_MODEL, IN_WIDTH), D_MODEL),
        "gate_b": 0.02 * jax.random.normal(ks[3], (DEPTH, N_BRANCH, D_MODEL), f32),
        "conv_dw": nrm(ks[4], (DEPTH, CONV_K, CONV_WIDTH), CONV_K),
        "conv_dw_b": 0.02 * jax.random.normal(ks[5], (DEPTH, CONV_WIDTH), f32),
        "conv_ln_g": gain(ks[6], (DEPTH, CONV_WIDTH)),
        "conv_ln_b": 0.02 * jax.random.normal(ks[7], (DEPTH, CONV_WIDTH), f32),
        "w_conv_out": nrm(ks[8], (DEPTH, CONV_WIDTH, D_MODEL), CONV_WIDTH),
        "q_norm_g": gain(ks[9], (DEPTH, ATT_HEAD_DIM)),
        "k_norm_g": gain(ks[10], (DEPTH, ATT_HEAD_DIM)),
        "w_att_out": nrm(ks[11], (DEPTH, ATT_WIDTH, D_MODEL), ATT_WIDTH),
        "pool_w": nrm(ks[12], (DEPTH, POOL_GROUPS, POOL_GROUP_DIM, POOL_GROUP_DIM), POOL_GROUP_DIM),
        "pool_scale": gain(ks[13], (DEPTH, POOL_WIDTH)),
        "w_pool_out": nrm(ks[14], (DEPTH, POOL_WIDTH, D_MODEL), POOL_WIDTH),
        "w_o": nrm(ks[15], (DEPTH, D_MODEL, D_MODEL), D_MODEL),
        "mlp_norm_g": gain(ks[16], (DEPTH, D_MODEL)),
        "w_mlp_in": nrm(ks[17], (DEPTH, D_MODEL, D_FF), D_MODEL),
        "w_mlp_out": nrm(ks[18], (DEPTH, D_FF, D_MODEL), D_FF),
    }


def reference(x, mix_norm_g, w_in, gate_b, conv_dw, conv_dw_b, conv_ln_g, conv_ln_b, w_conv_out,
              q_norm_g, k_norm_g, w_att_out, pool_w, pool_scale, w_pool_out, w_o,
              mlp_norm_g, w_mlp_in, w_mlp_out):
    B, S, D = x.shape
    assert S % CHUNK == 0 and S % Q_BLOCK == 0
    split_pts = list(np.cumsum(IN_SPLITS)[:-1])
    h = x
    for l in range(DEPTH):
        xn = rms_norm(h, mix_norm_g[l])
        proj = xn @ w_in[l]
        q, k, v, u_conv, p_pool, g_pre = jnp.split(proj, split_pts, axis=-1)

        y_conv = conv_module(u_conv, conv_dw[l], conv_dw_b[l], conv_ln_g[l], conv_ln_b[l], w_conv_out[l])

        def heads(t):
            return t.reshape(B, S, ATT_HEADS, ATT_HEAD_DIM).transpose(0, 2, 1, 3)
        qh = rms_norm(heads(q), q_norm_g[l])
        kh = rms_norm(heads(k), k_norm_g[l])
        o = stick_breaking_attention(qh, kh, heads(v))
        y_att = o.transpose(0, 2, 1, 3).reshape(B, S, ATT_WIDTH) @ w_att_out[l]

        y_pool = multiscale_pool(p_pool, pool_w[l], pool_scale[l], w_pool_out[l])

        gates = jax.nn.sigmoid(g_pre.reshape(B, S, N_BRANCH, D).astype(jnp.float32)
                               + gate_b[l].astype(jnp.float32)).astype(h.dtype)
        merged = gates[:, :, 0] * y_conv + gates[:, :, 1] * y_att + gates[:, :, 2] * y_pool
        h = h + merged @ w_o[l]

        hn = rms_norm(h, mlp_norm_g[l])
        ff = jnp.square(jax.nn.relu(hn @ w_mlp_in[l]))
        h = h + ff @ w_mlp_out[l]
    return h
```

```python
import functools

import jax
import jax.numpy as jnp
from jax import lax
from jax.experimental import pallas as pl
from jax.experimental.pallas import tpu as pltpu

F32 = jnp.float32
BF16 = jnp.bfloat16

EPS = 1e-6
CONV_K = 31
HEAD_DIM = 64
LANES = 128
POOL_WINDOWS = (2, 4, 8, 16)
CONV_HALO = 32
POOL_HALO = 16
STICK_EXHAUSTED = -105.0
VMEM_LIMIT = 56 * 1024 * 1024


def _dot(a, b):
    return jnp.dot(a, b, preferred_element_type=F32)


def _split_bf16(x):
    hi = x.astype(BF16)
    lo = (x - hi.astype(F32)).astype(BF16)
    return hi, lo


def _const_spec(shape):
    nd = len(shape)
    return pl.BlockSpec(shape, lambda *_: (0,) * nd)


def _inproj_kernel(x_ref, g_ref, w_ref, qg_ref, kg_ref, gsum_ref,
                   q_ref, k_ref, v_ref, u_ref, p_ref, gt_ref):
    x = x_ref[...]
    ms = jnp.mean(x * x, axis=-1, keepdims=True)
    xn = ((x * lax.rsqrt(ms + EPS)) * g_ref[...]).astype(BF16)
    aw = q_ref.shape[1]

    def head_norm(t, gain):
        hi, lo = _split_bf16(t * t)
        ss = _dot(hi, gsum_ref[...]) + _dot(lo, gsum_ref[...])
        return (t * lax.rsqrt(ss * (1.0 / HEAD_DIM) + EPS)) * gain

    q = head_norm(_dot(xn, w_ref[:, 0:aw]), qg_ref[...])
    q_ref[...] = (q * (HEAD_DIM ** -0.5)).astype(BF16)
    k = head_norm(_dot(xn, w_ref[:, aw:2 * aw]), kg_ref[...])
    k_ref[...] = k.astype(BF16)
    off = 2 * aw
    for ref in (v_ref, u_ref, p_ref, gt_ref):
        width = ref.shape[1]
        for c in range(0, width, 512):
            ref[:, c:c + 512] = _dot(xn, w_ref[:, off + c:off + c + 512]).astype(BF16)
        off += width


def _inproj(h, g, w, qg, kg, gsum, widths, tm):
    t, d = h.shape
    aw, cw, pw, gw = widths
    outs = [(t, aw), (t, aw), (t, aw), (t, cw), (t, pw), (t, gw)]
    return pl.pallas_call(
        _inproj_kernel,
        out_shape=[jax.ShapeDtypeStruct(s, BF16) for s in outs],
        grid=(t // tm,),
        in_specs=[pl.BlockSpec((tm, d), lambda i: (i, 0)),
                  _const_spec(g.shape), _const_spec(w.shape), _const_spec(qg.shape),
                  _const_spec(kg.shape), _const_spec(gsum.shape)],
        out_specs=[pl.BlockSpec((tm, s[1]), lambda i: (i, 0)) for s in outs],
        compiler_params=pltpu.CompilerParams(dimension_semantics=("arbitrary",),
                                             vmem_limit_bytes=VMEM_LIMIT),
        name="inproj",
    )(h, g, w, qg, kg, gsum)


def _attn_kernel(q_ref, k_ref, v_ref, u2_ref, o_ref, qq_ref, carry_ref, acc_ref):
    i = pl.program_id(2)
    tq = q_ref.shape[0]
    tk = tq
    q = q_ref[...]
    head0 = lax.broadcasted_iota(jnp.int32, (tq, LANES), 1) < HEAD_DIM
    zero = jnp.zeros_like(q)
    qq_ref[0:tq, :] = jnp.where(head0, q, zero)
    qq_ref[tq:2 * tq, :] = jnp.where(head0, zero, q)

    def block(j, diag):
        k0 = pl.multiple_of(j * tk, tk)
        kb = k_ref[pl.ds(k0, tk), :]
        vb = v_ref[pl.ds(k0, tk), :]
        z = lax.dot_general(qq_ref[...], kb, (((1,), (1,)), ((), ())),
                            preferred_element_type=F32)
        sp = jnp.maximum(z, 0.0) + jnp.log(1.0 + jnp.exp(-jnp.abs(z)))
        if diag:
            row = lax.broadcasted_iota(jnp.int32, (2 * tq, tk), 0)
            row = jnp.where(row >= tq, row - tq, row)
            col = lax.broadcasted_iota(jnp.int32, (2 * tq, tk), 1)
            mask = col < row
            m = jnp.where(mask, -sp, 0.0)
        else:
            m = -sp
        hi, lo = _split_bf16(m)
        cs = _dot(jnp.concatenate([hi, lo], axis=1), u2_ref[...])
        if diag:
            excl = cs[:, :tk]
            carry_ref[...] = cs[:, tk:]
        else:
            carry = carry_ref[...]
            excl = cs[:, :tk] + carry
            carry_ref[...] = carry + cs[:, tk:]
        a = jnp.exp((z - sp) + excl)
        if diag:
            a = jnp.where(mask, a, 0.0)
        ab = a.astype(BF16)
        a2 = jnp.concatenate([ab[:tq], ab[tq:]], axis=1)
        vzero = jnp.zeros_like(vb)
        vcat = jnp.concatenate([jnp.where(head0, vb, vzero), jnp.where(head0, vzero, vb)], axis=0)
        pv = _dot(a2, vcat)
        if diag:
            acc_ref[...] = pv
        else:
            acc_ref[...] += pv

    block(i, True)

    def cond(state):
        j, done = state
        return jnp.logical_and(j >= 0, done == 0)

    def body(state):
        j, _ = state
        block(j, False)
        exhausted = jnp.max(carry_ref[...]) < STICK_EXHAUSTED
        return j - 1, exhausted.astype(jnp.int32)

    lax.while_loop(cond, body, (i - 1, jnp.int32(0)))
    o_ref[...] = acc_ref[...].astype(o_ref.dtype)


def _attention(q, k, v, u2, tq):
    b, s, aw = q.shape
    groups = aw // LANES
    return pl.pallas_call(
        _attn_kernel,
        out_shape=jax.ShapeDtypeStruct((b, s, aw), BF16),
        grid=(b, groups, s // tq),
        in_specs=[pl.BlockSpec((None, tq, LANES), lambda bi, gi, i: (bi, i, gi)),
                  pl.BlockSpec((None, s, LANES), lambda bi, gi, i: (bi, 0, gi)),
                  pl.BlockSpec((None, s, LANES), lambda bi, gi, i: (bi, 0, gi)),
                  _const_spec(u2.shape)],
        out_specs=pl.BlockSpec((None, tq, LANES), lambda bi, gi, i: (bi, i, gi)),
        scratch_shapes=[pltpu.VMEM((2 * tq, LANES), BF16),
                        pltpu.VMEM((2 * tq, LANES), F32),
                        pltpu.VMEM((tq, LANES), F32)],
        compiler_params=pltpu.CompilerParams(
            dimension_semantics=("arbitrary", "arbitrary", "arbitrary"),
            vmem_limit_bytes=VMEM_LIMIT),
        name="stickbreak_attn",
    )(q, k, v, u2)


def _tail_kernel(h_ref, u_ref, p_ref, gt_ref, o_ref,
                 dw_ref, dwb_ref, lng_ref, lnb_ref, wconv_ref,
                 poolw_ref, pscale_ref, wpool_ref, watt_ref, gb_ref, wo_ref,
                 out_ref, hbuf_ref, pbuf_ref, conv_ref):
    si = pl.program_id(1)
    ts, d = h_ref.shape
    cw = hbuf_ref.shape[1]
    pw = pbuf_ref.shape[1]

    @pl.when(si == 0)
    def _():
        hbuf_ref[0:CONV_HALO, :] = jnp.zeros((CONV_HALO, cw), F32)
        pbuf_ref[0:POOL_HALO, :] = jnp.zeros((POOL_HALO, pw), F32)

    u = u_ref[...].astype(F32)
    hbuf_ref[CONV_HALO:CONV_HALO + ts, :] = u[:, :cw] * jax.nn.sigmoid(u[:, cw:])
    rc = 128
    for r0 in range(0, ts, rc):
        for c0 in range(0, cw, LANES):
            acc = jnp.zeros((rc, LANES), F32)
            for tap in range(CONV_K):
                start = CONV_HALO - (CONV_K - 1) + tap + r0
                acc = acc + dw_ref[tap:tap + 1, c0:c0 + LANES] * hbuf_ref[start:start + rc, c0:c0 + LANES]
            conv_ref[r0:r0 + rc, c0:c0 + LANES] = acc
    conv = conv_ref[...] + dwb_ref[...]
    hbuf_ref[0:CONV_HALO, :] = hbuf_ref[ts:ts + CONV_HALO, :]
    mu = jnp.mean(conv, axis=-1, keepdims=True)
    xc = conv - mu
    ln = (xc * lax.rsqrt(jnp.mean(xc * xc, axis=-1, keepdims=True) + EPS)) * lng_ref[...] + lnb_ref[...]
    y_conv = _dot((ln * jax.nn.sigmoid(ln)).astype(BF16), wconv_ref[...])

    pbuf_ref[POOL_HALO:POOL_HALO + ts, :] = p_ref[...].astype(F32)
    tpos = si * ts + lax.broadcasted_iota(jnp.int32, (ts, LANES), 0)
    pooled = []
    for g, w in enumerate(POOL_WINDOWS):
        c0 = g * LANES
        xg = pbuf_ref[POOL_HALO:POOL_HALO + ts, c0:c0 + LANES]
        tot = xg
        for back in range(1, w):
            tot = tot + pbuf_ref[POOL_HALO - back:POOL_HALO - back + ts, c0:c0 + LANES]
        count = jnp.minimum(tpos + 1, w).astype(F32)
        yg = (tot / count - xg).astype(BF16)
        pooled.append(_dot(yg, poolw_ref[g]))
    pbuf_ref[0:POOL_HALO, :] = pbuf_ref[ts:ts + POOL_HALO, :]
    y_pool = jnp.concatenate(pooled, axis=1) * pscale_ref[...]
    y_pool = _dot(y_pool.astype(BF16), wpool_ref[...])

    y_att = _dot(o_ref[...], watt_ref[...])
    merged = None
    for bidx, y in enumerate((y_conv, y_att, y_pool)):
        gate = jax.nn.sigmoid(gt_ref[:, bidx * d:(bidx + 1) * d].astype(F32) + gb_ref[bidx:bidx + 1, :])
        merged = gate * y if merged is None else merged + gate * y
    out_ref[...] = h_ref[...] + _dot(merged.astype(BF16), wo_ref[...])


def _mixer_tail(h, u, p, gt, o, weights, ts):
    b, s, d = h.shape
    cw = u.shape[2] // 2
    pw = p.shape[2]

    def tile(x):
        return pl.BlockSpec((None, ts, x.shape[2]), lambda bi, si: (bi, si, 0))

    return pl.pallas_call(
        _tail_kernel,
        out_shape=jax.ShapeDtypeStruct((b, s, d), F32),
        grid=(b, s // ts),
        in_specs=[tile(h), tile(u), tile(p), tile(gt), tile(o)] + [_const_spec(w.shape) for w in weights],
        out_specs=tile(h),
        scratch_shapes=[pltpu.VMEM((CONV_HALO + ts, cw), F32),
                        pltpu.VMEM((POOL_HALO + ts, pw), F32),
                        pltpu.VMEM((ts, cw), F32)],
        compiler_params=pltpu.CompilerParams(dimension_semantics=("arbitrary", "arbitrary"),
                                             vmem_limit_bytes=VMEM_LIMIT),
        name="mixer_tail",
    )(h, u, p, gt, o, *weights)


def _mlp_kernel(x_ref, g_ref, w1_ref, w2_ref, out_ref):
    x = x_ref[...]
    ms = jnp.mean(x * x, axis=-1, keepdims=True)
    xn = ((x * lax.rsqrt(ms + EPS)) * g_ref[...]).astype(BF16)
    dff = w1_ref.shape[1]
    fc = 1024
    acc = x
    for c in range(0, dff, fc):
        ff = jnp.maximum(_dot(xn, w1_ref[:, c:c + fc]), 0.0)
        acc = acc + _dot((ff * ff).astype(BF16), w2_ref[c:c + fc, :])
    out_ref[...] = acc


def _mlp(h, g, w1, w2, tm):
    t, d = h.shape
    return pl.pallas_call(
        _mlp_kernel,
        out_shape=jax.ShapeDtypeStruct((t, d), F32),
        grid=(t // tm,),
        in_specs=[pl.BlockSpec((tm, d), lambda i: (i, 0)),
                  _const_spec(g.shape), _const_spec(w1.shape), _const_spec(w2.shape)],
        out_specs=pl.BlockSpec((tm, d), lambda i: (i, 0)),
        compiler_params=pltpu.CompilerParams(dimension_semantics=("arbitrary",),
                                             vmem_limit_bytes=VMEM_LIMIT),
        name="relu2_mlp",
    )(h, g, w1, w2)


def _pick_tile(n, target):
    t = min(n, target)
    while n % t:
        t //= 2
    return t


def kernel(x, mix_norm_g, w_in, gate_b, conv_dw, conv_dw_b, conv_ln_g, conv_ln_b, w_conv_out,
           q_norm_g, k_norm_g, w_att_out, pool_w, pool_scale, w_pool_out, w_o,
           mlp_norm_g, w_mlp_in, w_mlp_out):
    b, s, d = x.shape
    depth = w_in.shape[0]
    aw = w_att_out.shape[1]
    cw = w_conv_out.shape[1]
    pw = w_pool_out.shape[1]
    widths = (aw, 2 * cw, pw, gate_b.shape[1] * d)
    tq = LANES
    assert s % tq == 0 and aw % LANES == 0 and LANES == 2 * HEAD_DIM
    tm = _pick_tile(b * s, 512)
    ts = _pick_tile(s, 512)

    lane = jnp.arange(aw)
    gsum = (lane[:, None] // HEAD_DIM == lane[None, :] // HEAD_DIM).astype(BF16)
    r = jnp.arange(2 * tq)
    key = r % tq
    u2 = jnp.where(r[None, :] < tq, key[:, None] > r[None, :], True).astype(BF16)

    h = x
    for l in range(depth):
        qg = jnp.tile(q_norm_g[l], aw // HEAD_DIM)[None, :]
        kg = jnp.tile(k_norm_g[l], aw // HEAD_DIM)[None, :]
        q, k, v, u, p, gt = _inproj(h.reshape(b * s, d), mix_norm_g[l][None, :], w_in[l].astype(BF16),
                                    qg, kg, gsum, widths, tm)
        o = _attention(q.reshape(b, s, aw), k.reshape(b, s, aw), v.reshape(b, s, aw), u2, tq)
        weights = (conv_dw[l], conv_dw_b[l][None, :], conv_ln_g[l][None, :], conv_ln_b[l][None, :],
                   w_conv_out[l].astype(BF16), pool_w[l].astype(BF16), pool_scale[l][None, :],
                   w_pool_out[l].astype(BF16), w_att_out[l].astype(BF16), gate_b[l],
                   w_o[l].astype(BF16))
        h = _mixer_tail(h, u.reshape(b, s, 2 * cw), p.reshape(b, s, pw), gt.reshape(b, s, -1), o,
                        weights, ts)
        h = _mlp(h.reshape(b * s, d), mlp_norm_g[l][None, :], w_mlp_in[l].astype(BF16),
                 w_mlp_out[l].astype(BF16), tm).reshape(b, s, d)
    return h
```

```python
import functools
import math

import jax
import jax.numpy as jnp
from jax import lax
from jax.experimental import pallas as pl
from jax.experimental.pallas import tpu as pltpu

F32 = jnp.float32
BF16 = jnp.bfloat16

EPS = 1e-6
CONV_K = 31
HEAD_DIM = 64
LANES = 128
SUBLANES = 8
POOL_WINDOWS = (2, 4, 8, 16)
CONV_HALO = 32
POOL_HALO = 16
ATT_BLK = LANES
ATT_WINDOW = 3
LOG2E = math.log2(math.e)
STICK_EXHAUSTED_LOG2 = -105.0 * LOG2E
VMEM_LIMIT = 56 * 1024 * 1024


def _dot(a, b):
    return jnp.dot(a, b, preferred_element_type=F32)


def _split_bf16(x):
    hi = x.astype(BF16)
    lo = (x - hi.astype(F32)).astype(BF16)
    return hi, lo


def _const_spec(shape):
    nd = len(shape)
    return pl.BlockSpec(shape, lambda *_: (0,) * nd)


def _inproj_kernel(x_ref, g_ref, w_ref, qg_ref, kg_ref, gsum_ref, gb_ref,
                   q_ref, k_ref, v_ref, hc_ref, p_ref, gt_ref):
    x = x_ref[...]
    ms = jnp.mean(x * x, axis=-1, keepdims=True)
    xn = ((x * lax.rsqrt(ms + EPS)) * g_ref[...]).astype(BF16)
    aw = q_ref.shape[1]
    cw = hc_ref.shape[1]
    pw = p_ref.shape[1]

    def proj(start, width):
        return _dot(xn, w_ref[:, start:start + width])

    def head_norm(t, gain):
        ss = _dot((t * t).astype(BF16), gsum_ref[...])
        return (t * lax.rsqrt(ss * (1.0 / HEAD_DIM) + EPS)) * gain

    q_ref[...] = (head_norm(proj(0, aw), qg_ref[...]) * (HEAD_DIM ** -0.5 * LOG2E)).astype(BF16)
    k_ref[...] = head_norm(proj(aw, aw), kg_ref[...]).astype(BF16)
    v_ref[...] = proj(2 * aw, aw).astype(BF16)
    off = 3 * aw
    hc_ref[...] = (proj(off, cw) * jax.nn.sigmoid(proj(off + cw, cw))).astype(BF16)
    off += 2 * cw
    p_ref[...] = proj(off, pw).astype(BF16)
    off += pw
    for c in range(0, gt_ref.shape[1], 512):
        gt_ref[:, c:c + 512] = jax.nn.sigmoid(proj(off + c, 512) + gb_ref[:, c:c + 512]).astype(BF16)


def _inproj(h, g, w, qg, kg, gsum, gate_b, widths, tm):
    t, d = h.shape
    aw, cw, pw, gw = widths
    outs = [(t, aw), (t, aw), (t, aw), (t, cw), (t, pw), (t, gw)]
    consts = (g, w, qg, kg, gsum, gate_b)
    return pl.pallas_call(
        _inproj_kernel,
        out_shape=[jax.ShapeDtypeStruct(s, BF16) for s in outs],
        grid=(t // tm,),
        in_specs=[pl.BlockSpec((tm, d), lambda i: (i, 0))] + [_const_spec(c.shape) for c in consts],
        out_specs=[pl.BlockSpec((tm, s[1]), lambda i: (i, 0)) for s in outs],
        compiler_params=pltpu.CompilerParams(dimension_semantics=("arbitrary",),
                                             vmem_limit_bytes=VMEM_LIMIT),
        name="inproj",
    )(h, *consts)


def _attn_kernel(q_ref, k_ref, v_ref, u2_ref, o_ref, qq_ref, carry_ref, acc_ref):
    qi = pl.program_id(2)
    blk = ATT_BLK
    nq = q_ref.shape[0] // blk
    head0 = lax.broadcasted_iota(jnp.int32, (blk, LANES), 1) < HEAD_DIM
    for n in range(nq):
        q = q_ref[n * blk:(n + 1) * blk, :]
        zero = jnp.zeros_like(q)
        qq_ref[n, 0:blk, :] = jnp.where(head0, q, zero)
        qq_ref[n, blk:2 * blk, :] = jnp.where(head0, zero, q)

    row = lax.broadcasted_iota(jnp.int32, (2 * blk, blk), 0)
    row = jnp.where(row >= blk, row - blk, row)
    causal = lax.broadcasted_iota(jnp.int32, (2 * blk, blk), 1) < row

    def stage_a(n, j, diag):
        k0 = pl.multiple_of(j * blk, blk)
        z = lax.dot_general(qq_ref[n], k_ref[pl.ds(k0, blk), :], (((1,), (1,)), ((), ())),
                            preferred_element_type=F32)
        lb = jnp.minimum(z, 0.0) - jnp.log2(1.0 + jnp.exp2(-jnp.abs(z)))
        m = lb - z
        if diag:
            m = jnp.where(causal, m, 0.0)
        hi, lo = _split_bf16(m)
        cs = _dot(jnp.concatenate([hi, lo], axis=1), u2_ref[...])
        return lb, cs[:, :blk], cs[:, blk:]

    def stage_b(lb, excl, diag):
        a = jnp.exp2(lb + excl)
        if diag:
            a = jnp.where(causal, a, 0.0)
        ab = a.astype(BF16)
        return jnp.concatenate([ab[:blk], ab[blk:]], axis=1)

    def v_stack(j):
        k0 = pl.multiple_of(j * blk, blk)
        vb = v_ref[pl.ds(k0, blk), :]
        vzero = jnp.zeros_like(vb)
        return jnp.concatenate([jnp.where(head0, vb, vzero), jnp.where(head0, vzero, vb)], axis=0)

    @pl.when(qi > 0)
    def _():
        for n in range(nq):
            i = qi * nq + n
            weights, values, run = [], [], None
            for back in range(ATT_WINDOW):
                lb, excl, tot = stage_a(n, i - back, back == 0)
                if run is not None:
                    excl = excl + run
                run = tot if run is None else run + tot
                weights.append(stage_b(lb, excl, back == 0))
                values.append(v_stack(i - back))
            acc_ref[n] = _dot(jnp.concatenate(weights, axis=1), jnp.concatenate(values, axis=0))
            carry_ref[n] = run

    @pl.when(qi == 0)
    def _():
        for n in range(nq):
            lb, excl, tot = stage_a(n, n, True)
            acc_ref[n] = _dot(stage_b(lb, excl, True), v_stack(n))
            carry_ref[n] = tot

    swept = jnp.where(qi == 0, 1, ATT_WINDOW)

    @pl.when(jnp.max(carry_ref[...]) >= STICK_EXHAUSTED_LOG2)
    def _():
        for n in range(nq):
            def cond(state):
                j, done = state
                return jnp.logical_and(j >= 0, done == 0)

            def body(state, n=n):
                j, _ = state
                lb, excl, tot = stage_a(n, j, False)
                carry = carry_ref[n]
                acc_ref[n] += _dot(stage_b(lb, excl + carry, False), v_stack(j))
                carry = carry + tot
                carry_ref[n] = carry
                return j - 1, (jnp.max(carry) < STICK_EXHAUSTED_LOG2).astype(jnp.int32)

            done0 = (jnp.max(carry_ref[n]) < STICK_EXHAUSTED_LOG2).astype(jnp.int32)
            lax.while_loop(cond, body, (qi * nq + n - swept, done0))

    for n in range(nq):
        o_ref[n * blk:(n + 1) * blk, :] = acc_ref[n].astype(o_ref.dtype)


def _attention(q, k, v, u2, tq):
    b, s, aw = q.shape
    groups = aw // LANES
    nq = tq // ATT_BLK
    assert nq >= ATT_WINDOW - 1
    return pl.pallas_call(
        _attn_kernel,
        out_shape=jax.ShapeDtypeStruct((b, s, aw), BF16),
        grid=(b, groups, s // tq),
        in_specs=[pl.BlockSpec((None, tq, LANES), lambda bi, gi, i: (bi, i, gi)),
                  pl.BlockSpec((None, s, LANES), lambda bi, gi, i: (bi, 0, gi)),
                  pl.BlockSpec((None, s, LANES), lambda bi, gi, i: (bi, 0, gi)),
                  _const_spec(u2.shape)],
        out_specs=pl.BlockSpec((None, tq, LANES), lambda bi, gi, i: (bi, i, gi)),
        scratch_shapes=[pltpu.VMEM((nq, 2 * ATT_BLK, LANES), BF16),
                        pltpu.VMEM((nq, 2 * ATT_BLK, LANES), F32),
                        pltpu.VMEM((nq, ATT_BLK, LANES), F32)],
        compiler_params=pltpu.CompilerParams(
            dimension_semantics=("arbitrary", "arbitrary", "arbitrary"),
            vmem_limit_bytes=VMEM_LIMIT),
        name="stickbreak_attn",
    )(q, k, v, u2)


def _causal_depthwise_conv(hbuf_ref, dw_ref, conv_ref, ts):
    cw = hbuf_ref.shape[1]
    rc = 128
    for r0 in range(0, ts, rc):
        for c0 in range(0, cw, LANES):
            acc = None
            for b in range(SUBLANES):
                part = None
                for a in range((CONV_K - 1 - b) // SUBLANES + 1):
                    tap = CONV_K - 1 - (SUBLANES * a + b)
                    start = CONV_HALO - SUBLANES + r0 - SUBLANES * a
                    term = dw_ref[tap:tap + 1, c0:c0 + LANES] * hbuf_ref[start:start + rc + SUBLANES,
                                                                         c0:c0 + LANES]
                    part = term if part is None else part + term
                if b:
                    part = pltpu.roll(part, b, axis=0)
                part = part[SUBLANES:, :]
                acc = part if acc is None else acc + part
            conv_ref[r0:r0 + rc, c0:c0 + LANES] = acc


def _tail_kernel(h_ref, hc_ref, p_ref, gt_ref, o_ref,
                 dw_ref, dwb_ref, lng_ref, lnb_ref, wconv_ref,
                 poolw_ref, pscale_ref, wpool_ref, watt_ref, wo_ref,
                 out_ref, hbuf_ref, pbuf_ref, conv_ref):
    si = pl.program_id(1)
    ts, d = h_ref.shape
    cw = hbuf_ref.shape[1]
    pw = pbuf_ref.shape[1]

    @pl.when(si == 0)
    def _():
        hbuf_ref[0:CONV_HALO, :] = jnp.zeros((CONV_HALO, cw), F32)
        pbuf_ref[0:POOL_HALO, :] = jnp.zeros((POOL_HALO, pw), F32)

    hbuf_ref[CONV_HALO:CONV_HALO + ts, :] = hc_ref[...].astype(F32)
    _causal_depthwise_conv(hbuf_ref, dw_ref, conv_ref, ts)
    conv = conv_ref[...] + dwb_ref[...]
    hbuf_ref[0:CONV_HALO, :] = hbuf_ref[ts:ts + CONV_HALO, :]
    mu = jnp.mean(conv, axis=-1, keepdims=True)
    xc = conv - mu
    ln = (xc * lax.rsqrt(jnp.mean(xc * xc, axis=-1, keepdims=True) + EPS)) * lng_ref[...] + lnb_ref[...]
    y_conv = _dot((ln * jax.nn.sigmoid(ln)).astype(BF16), wconv_ref[...])

    pbuf_ref[POOL_HALO:POOL_HALO + ts, :] = p_ref[...].astype(F32)
    tpos = si * ts + lax.broadcasted_iota(jnp.int32, (ts, LANES), 0)
    pooled = []
    for g, w in enumerate(POOL_WINDOWS):
        c0 = g * LANES
        xg = pbuf_ref[POOL_HALO:POOL_HALO + ts, c0:c0 + LANES]
        tot = xg
        for back in range(1, w):
            tot = tot + pbuf_ref[POOL_HALO - back:POOL_HALO - back + ts, c0:c0 + LANES]
        count = jnp.minimum(tpos + 1, w).astype(F32)
        yg = (tot / count - xg).astype(BF16)
        pooled.append(_dot(yg, poolw_ref[g]))
    pbuf_ref[0:POOL_HALO, :] = pbuf_ref[ts:ts + POOL_HALO, :]
    y_pool = jnp.concatenate(pooled, axis=1) * pscale_ref[...]
    y_pool = _dot(y_pool.astype(BF16), wpool_ref[...])

    y_att = _dot(o_ref[...], watt_ref[...])
    merged = None
    for bidx, y in enumerate((y_conv, y_att, y_pool)):
        gate = gt_ref[:, bidx * d:(bidx + 1) * d].astype(F32)
        merged = gate * y if merged is None else merged + gate * y
    out_ref[...] = h_ref[...] + _dot(merged.astype(BF16), wo_ref[...])


def _mixer_tail(h, hc, p, gt, o, weights, ts):
    b, s, d = h.shape
    cw = hc.shape[2]
    pw = p.shape[2]

    def tile(x):
        return pl.BlockSpec((None, ts, x.shape[2]), lambda bi, si: (bi, si, 0))

    return pl.pallas_call(
        _tail_kernel,
        out_shape=jax.ShapeDtypeStruct((b, s, d), F32),
        grid=(b, s // ts),
        in_specs=[tile(h), tile(hc), tile(p), tile(gt), tile(o)] + [_const_spec(w.shape) for w in weights],
        out_specs=tile(h),
        scratch_shapes=[pltpu.VMEM((CONV_HALO + ts, cw), F32),
                        pltpu.VMEM((POOL_HALO + ts, pw), F32),
                        pltpu.VMEM((ts, cw), F32)],
        compiler_params=pltpu.CompilerParams(dimension_semantics=("arbitrary", "arbitrary"),
                                             vmem_limit_bytes=VMEM_LIMIT),
        name="mixer_tail",
    )(h, hc, p, gt, o, *weights)


def _mlp_kernel(x_ref, g_ref, w1_ref, w2_ref, out_ref):
    x = x_ref[...]
    ms = jnp.mean(x * x, axis=-1, keepdims=True)
    xn = ((x * lax.rsqrt(ms + EPS)) * g_ref[...]).astype(BF16)
    dff = w1_ref.shape[1]
    fc = 1024
    acc = x
    for c in range(0, dff, fc):
        ff = jnp.maximum(_dot(xn, w1_ref[:, c:c + fc]), 0.0)
        acc = acc + _dot((ff * ff).astype(BF16), w2_ref[c:c + fc, :])
    out_ref[...] = acc


def _mlp(h, g, w1, w2, tm):
    t, d = h.shape
    return pl.pallas_call(
        _mlp_kernel,
        out_shape=jax.ShapeDtypeStruct((t, d), F32),
        grid=(t // tm,),
        in_specs=[pl.BlockSpec((tm, d), lambda i: (i, 0)),
                  _const_spec(g.shape), _const_spec(w1.shape), _const_spec(w2.shape)],
        out_specs=pl.BlockSpec((tm, d), lambda i: (i, 0)),
        compiler_params=pltpu.CompilerParams(dimension_semantics=("arbitrary",),
                                             vmem_limit_bytes=VMEM_LIMIT),
        name="relu2_mlp",
    )(h, g, w1, w2)


def _pick_tile(n, target):
    t = min(n, target)
    while n % t:
        t //= 2
    return t


def kernel(x, mix_norm_g, w_in, gate_b, conv_dw, conv_dw_b, conv_ln_g, conv_ln_b, w_conv_out,
           q_norm_g, k_norm_g, w_att_out, pool_w, pool_scale, w_pool_out, w_o,
           mlp_norm_g, w_mlp_in, w_mlp_out):
    b, s, d = x.shape
    depth = w_in.shape[0]
    aw = w_att_out.shape[1]
    cw = w_conv_out.shape[1]
    pw = w_pool_out.shape[1]
    widths = (aw, cw, pw, gate_b.shape[1] * d)
    assert s % ATT_BLK == 0 and aw % LANES == 0 and LANES == 2 * HEAD_DIM
    tm = _pick_tile(b * s, 512)
    ts = _pick_tile(s, 512)
    tq = _pick_tile(s, 4 * ATT_BLK)

    lane = jnp.arange(aw)
    gsum = (lane[:, None] // HEAD_DIM == lane[None, :] // HEAD_DIM).astype(BF16)
    r = jnp.arange(2 * ATT_BLK)
    key = r % ATT_BLK
    u2 = jnp.where(r[None, :] < ATT_BLK, key[:, None] > r[None, :], True).astype(BF16)

    h = x
    for l in range(depth):
        qg = jnp.tile(q_norm_g[l], aw // HEAD_DIM)[None, :]
        kg = jnp.tile(k_norm_g[l], aw // HEAD_DIM)[None, :]
        q, k, v, hc, p, gt = _inproj(h.reshape(b * s, d), mix_norm_g[l][None, :], w_in[l].astype(BF16),
                                     qg, kg, gsum, gate_b[l].reshape(1, -1), widths, tm)
        o = _attention(q.reshape(b, s, aw), k.reshape(b, s, aw), v.reshape(b, s, aw), u2, tq)
        weights = (conv_dw[l], conv_dw_b[l][None, :], conv_ln_g[l][None, :], conv_ln_b[l][None, :],
                   w_conv_out[l].astype(BF16), pool_w[l].astype(BF16), pool_scale[l][None, :],
                   w_pool_out[l].astype(BF16), w_att_out[l].astype(BF16), w_o[l].astype(BF16))
        h = _mixer_tail(h, hc.reshape(b, s, cw), p.reshape(b, s, pw), gt.reshape(b, s, -1), o,
                        weights, ts)
        h = _mlp(h.reshape(b * s, d), mlp_norm_g[l][None, :], w_mlp_in[l].astype(BF16),
                 w_mlp_out[l].astype(BF16), tm).reshape(b, s, d)
    return h
```

```python
import functools
import math

import jax
import jax.numpy as jnp
from jax import lax
from jax.experimental import pallas as pl
from jax.experimental.pallas import tpu as pltpu

F32 = jnp.float32
BF16 = jnp.bfloat16

EPS = 1e-6
CONV_K = 31
HEAD_DIM = 64
LANES = 128
SUBLANES = 8
POOL_WINDOWS = (2, 4, 8, 16)
CONV_HALO = 32
POOL_HALO = 16
ATT_BLK = LANES
ATT_WINDOW = 3
ATT_STEP_BLOCKS = 16
TOKEN_TILE = 1024
ROW_SLABS = 4
LOG2E = math.log2(math.e)
STICK_EXHAUSTED_LOG2 = -105.0 * LOG2E
VMEM_LIMIT = 56 * 1024 * 1024


def _dot(a, b):
    return jnp.dot(a, b, preferred_element_type=F32)


def _split_bf16(x):
    hi = x.astype(BF16)
    lo = (x - hi.astype(F32)).astype(BF16)
    return hi, lo


def _tree_max(xs):
    while len(xs) > 1:
        xs = [jnp.maximum(xs[i], xs[i + 1]) if i + 1 < len(xs) else xs[i] for i in range(0, len(xs), 2)]
    return xs[0]


def _const_spec(shape):
    nd = len(shape)
    return pl.BlockSpec(shape, lambda *_: (0,) * nd, pipeline_mode=pl.Buffered(1))


def _inproj_kernel(x_ref, g_ref, w_ref, qg_ref, kg_ref, gsum_ref, gb_ref,
                   q_ref, k_ref, v_ref, hc_ref, p_ref, gt_ref):
    aw = q_ref.shape[1]
    cw = hc_ref.shape[1]
    pw = p_ref.shape[1]
    off_c = 3 * aw
    off_p = off_c + 2 * cw
    off_g = off_p + pw
    rows_per = x_ref.shape[0] // ROW_SLABS

    def normed(r0):
        x = x_ref[r0:r0 + rows_per, :]
        ms = jnp.mean(x * x, axis=-1, keepdims=True)
        return ((x * lax.rsqrt(ms + EPS)) * g_ref[...]).astype(BF16)

    def head_norm(t, gain):
        ss = _dot((t * t).astype(BF16), gsum_ref[...])
        return (t * lax.rsqrt(ss * (1.0 / HEAD_DIM) + EPS)) * gain

    def stages(r0):
        rs = slice(r0, r0 + rows_per)

        def emit_q(t):
            q_ref[rs, :] = (head_norm(t, qg_ref[...]) * (HEAD_DIM ** -0.5 * LOG2E)).astype(BF16)

        def emit_k(t):
            k_ref[rs, :] = head_norm(t, kg_ref[...]).astype(BF16)

        def emit_v(t):
            v_ref[rs, :] = t.astype(BF16)

        def emit_conv(t):
            hc_ref[rs, :] = (t[:, :cw] * jax.nn.sigmoid(t[:, cw:])).astype(BF16)

        def emit_p(t):
            p_ref[rs, :] = t.astype(BF16)

        def emit_gate(c):
            def emit(t):
                gt_ref[rs, c:c + 512] = jax.nn.sigmoid(t + gb_ref[:, c:c + 512]).astype(BF16)
            return emit

        out = [(0, aw, emit_q), (aw, aw, emit_k), (2 * aw, aw, emit_v), (off_c, 2 * cw, emit_conv),
               (off_p, pw, emit_p)]
        out += [(off_g + c, 512, emit_gate(c)) for c in range(0, gt_ref.shape[1], 512)]
        return out

    xn = normed(0)
    for sidx in range(ROW_SLABS):
        xn_next = None
        todo = stages(sidx * rows_per)
        for c, (start, width, emit) in enumerate(todo):
            emit(_dot(xn, w_ref[:, start:start + width]))
            if c == len(todo) // 2 and sidx + 1 < ROW_SLABS:
                xn_next = normed((sidx + 1) * rows_per)
        xn = xn_next


def _inproj(h, g, w, qg, kg, gsum, gate_b, widths, tm):
    t, d = h.shape
    aw, cw, pw, gw = widths
    outs = [(t, aw), (t, aw), (t, aw), (t, cw), (t, pw), (t, gw)]
    consts = (g, w, qg, kg, gsum, gate_b)
    return pl.pallas_call(
        _inproj_kernel,
        out_shape=[jax.ShapeDtypeStruct(s, BF16) for s in outs],
        grid=(t // tm,),
        in_specs=[pl.BlockSpec((tm, d), lambda i: (i, 0))] + [_const_spec(c.shape) for c in consts],
        out_specs=[pl.BlockSpec((tm, s[1]), lambda i: (i, 0)) for s in outs],
        compiler_params=pltpu.CompilerParams(dimension_semantics=("arbitrary",),
                                             vmem_limit_bytes=VMEM_LIMIT),
        name="inproj",
    )(h, *consts)


def _attn_kernel(q_ref, k_ref, v_ref, u2_ref, o_ref, carry_ref, acc_ref):
    qi = pl.program_id(2)
    blk = ATT_BLK
    nq = q_ref.shape[0] // blk
    head0 = lax.broadcasted_iota(jnp.int32, (blk, LANES), 1) < HEAD_DIM
    causal = (lax.broadcasted_iota(jnp.int32, (blk, 2 * blk), 1) & (blk - 1)
              < lax.broadcasted_iota(jnp.int32, (blk, 2 * blk), 0))

    def head_stack(ref, j):
        x = ref[pl.ds(pl.multiple_of(j * blk, blk), blk), :]
        zero = jnp.zeros_like(x)
        return jnp.concatenate([jnp.where(head0, x, zero), jnp.where(head0, zero, x)], axis=0)

    def logits(q_rows, k_stack):
        return lax.dot_general(q_rows, k_stack, (((1,), (1,)), ((), ())),
                               preferred_element_type=F32)

    def log_terms(z, diag):
        neg_abs = pltpu.bitcast(pltpu.bitcast(z, jnp.uint32) | jnp.uint32(0x80000000), F32)
        lb = jnp.minimum(z, 0.0) - jnp.log2(1.0 + jnp.exp2(neg_abs))
        m = lb - z
        if diag:
            m = jnp.where(causal, m, 0.0)
        return lb, m

    def block_cumsums(ms):
        rows = []
        for m in ms:
            hi, lo = _split_bf16(m)
            for h in range(2):
                rows.append(jnp.concatenate([hi[:, h * blk:(h + 1) * blk], lo[:, h * blk:(h + 1) * blk]], axis=1))
        cs = _dot(jnp.concatenate(rows, axis=0), u2_ref[...])
        out = []
        for p in range(len(ms)):
            c0 = cs[(2 * p) * blk:(2 * p + 1) * blk]
            c1 = cs[(2 * p + 1) * blk:(2 * p + 2) * blk]
            out.append((jnp.concatenate([c0[:, :blk], c1[:, :blk]], axis=1),
                        jnp.concatenate([c0[:, blk:], c1[:, blk:]], axis=1)))
        return out

    def weights(lb, excl, diag):
        a = jnp.exp2(lb + excl)
        if diag:
            a = jnp.where(causal, a, 0.0)
        return a.astype(BF16)

    def stick_max(carries):
        rows = _tree_max(list(carries))
        return jnp.max(_tree_max([rows[r:r + SUBLANES] for r in range(0, blk, SUBLANES)]))

    def sweep(first_step):
        plan = [(n, range(min(ATT_WINDOW, n + 1) if first_step else ATT_WINDOW)) for n in range(nq)]
        rel = sorted({n - back for n, backs in plan for back in backs})
        kst = {r: head_stack(k_ref, qi * nq + r) for r in rel}
        vst = {r: head_stack(v_ref, qi * nq + r) for r in rel}
        zs = [[logits(q_ref[n * blk:(n + 1) * blk, :], kst[n - back]) for back in backs] for n, backs in plan]
        terms = [[log_terms(z, back == 0) for z, back in zip(zrow, backs)] for zrow, (n, backs) in zip(zs, plan)]
        sums = [block_cumsums([m for _, m in trow]) for trow in terms]
        probs, runs = [], []
        for trow, srow, (n, backs) in zip(terms, sums, plan):
            run, prow = None, []
            for back, (lb, _), (excl, tot) in zip(backs, trow, srow):
                if run is not None:
                    excl = excl + run
                run = tot if run is None else run + tot
                prow.append(weights(lb, excl, back == 0))
            probs.append(prow)
            runs.append(run)
        for prow, run, (n, backs) in zip(probs, runs, plan):
            acc_ref[n] = _dot(jnp.concatenate(prow, axis=1),
                              jnp.concatenate([vst[n - back] for back in backs], axis=0))
            carry_ref[n] = run

    @pl.when(qi > 0)
    def _():
        sweep(False)

    @pl.when(qi == 0)
    def _():
        sweep(True)

    @pl.when(stick_max([carry_ref[n] for n in range(nq)]) >= STICK_EXHAUSTED_LOG2)
    def _():
        for n in range(nq):
            def cond(state):
                j, done = state
                return jnp.logical_and(j >= 0, done == 0)

            def body(state, n=n):
                j, _ = state
                lb, m = log_terms(logits(q_ref[n * blk:(n + 1) * blk, :], head_stack(k_ref, j)), False)
                (excl, tot), = block_cumsums([m])
                carry = carry_ref[n]
                acc_ref[n] += _dot(weights(lb, excl + carry, False), head_stack(v_ref, j))
                carry = carry + tot
                carry_ref[n] = carry
                return j - 1, (stick_max([carry]) < STICK_EXHAUSTED_LOG2).astype(jnp.int32)

            done0 = (stick_max([carry_ref[n]]) < STICK_EXHAUSTED_LOG2).astype(jnp.int32)
            lax.while_loop(cond, body, (qi * nq + n - ATT_WINDOW, done0))

    for n in range(nq):
        o_ref[n * blk:(n + 1) * blk, :] = acc_ref[n].astype(o_ref.dtype)


def _attention(q, k, v, u2, tq):
    b, s, aw = q.shape
    groups = aw // LANES
    nq = tq // ATT_BLK
    assert nq >= ATT_WINDOW - 1
    return pl.pallas_call(
        _attn_kernel,
        out_shape=jax.ShapeDtypeStruct((b, s, aw), BF16),
        grid=(b, groups, s // tq),
        in_specs=[pl.BlockSpec((None, tq, LANES), lambda bi, gi, i: (bi, i, gi)),
                  pl.BlockSpec((None, s, LANES), lambda bi, gi, i: (bi, 0, gi)),
                  pl.BlockSpec((None, s, LANES), lambda bi, gi, i: (bi, 0, gi)),
                  _const_spec(u2.shape)],
        out_specs=pl.BlockSpec((None, tq, LANES), lambda bi, gi, i: (bi, i, gi)),
        scratch_shapes=[pltpu.VMEM((nq, ATT_BLK, 2 * ATT_BLK), F32),
                        pltpu.VMEM((nq, ATT_BLK, LANES), F32)],
        compiler_params=pltpu.CompilerParams(
            dimension_semantics=("arbitrary", "arbitrary", "arbitrary"),
            vmem_limit_bytes=VMEM_LIMIT),
        name="stickbreak_attn",
    )(q, k, v, u2)


def _causal_depthwise_conv(hbuf_ref, dw_ref, conv_ref, ts):
    cw = hbuf_ref.shape[1]
    rc = 128
    for r0 in range(0, ts, rc):
        for c0 in range(0, cw, LANES):
            acc = None
            for b in range(SUBLANES):
                part = None
                for a in range((CONV_K - 1 - b) // SUBLANES + 1):
                    tap = CONV_K - 1 - (SUBLANES * a + b)
                    start = CONV_HALO - SUBLANES + r0 - SUBLANES * a
                    term = dw_ref[tap:tap + 1, c0:c0 + LANES] * hbuf_ref[start:start + rc + SUBLANES,
                                                                         c0:c0 + LANES]
                    part = term if part is None else part + term
                if b:
                    part = pltpu.roll(part, b, axis=0)
                part = part[SUBLANES:, :]
                acc = part if acc is None else acc + part
            conv_ref[r0:r0 + rc, c0:c0 + LANES] = acc


def _tail_kernel(h_ref, hc_ref, p_ref, gt_ref, o_ref,
                 dw_ref, dwb_ref, lng_ref, lnb_ref, wconv_ref,
                 poolw_ref, pscale_ref, wpool_ref, watt_ref, wo_ref,
                 out_ref, hbuf_ref, pbuf_ref, conv_ref):
    si = pl.program_id(1)
    ts, d = h_ref.shape
    cw = hbuf_ref.shape[1]
    pw = pbuf_ref.shape[1]

    @pl.when(si == 0)
    def _():
        hbuf_ref[0:CONV_HALO, :] = jnp.zeros((CONV_HALO, cw), F32)
        pbuf_ref[0:POOL_HALO, :] = jnp.zeros((POOL_HALO, pw), F32)

    hbuf_ref[CONV_HALO:CONV_HALO + ts, :] = hc_ref[...].astype(F32)
    _causal_depthwise_conv(hbuf_ref, dw_ref, conv_ref, ts)
    conv = conv_ref[...] + dwb_ref[...]
    hbuf_ref[0:CONV_HALO, :] = hbuf_ref[ts:ts + CONV_HALO, :]
    mu = jnp.mean(conv, axis=-1, keepdims=True)
    xc = conv - mu
    ln = (xc * lax.rsqrt(jnp.mean(xc * xc, axis=-1, keepdims=True) + EPS)) * lng_ref[...] + lnb_ref[...]
    y_conv = _dot((ln * jax.nn.sigmoid(ln)).astype(BF16), wconv_ref[...])

    pbuf_ref[POOL_HALO:POOL_HALO + ts, :] = p_ref[...].astype(F32)
    tpos = si * ts + lax.broadcasted_iota(jnp.int32, (ts, LANES), 0)
    pooled = []
    for g, w in enumerate(POOL_WINDOWS):
        c0 = g * LANES
        xg = pbuf_ref[POOL_HALO:POOL_HALO + ts, c0:c0 + LANES]
        tot = xg
        for back in range(1, w):
            tot = tot + pbuf_ref[POOL_HALO - back:POOL_HALO - back + ts, c0:c0 + LANES]
        count = jnp.minimum(tpos + 1, w).astype(F32)
        yg = (tot / count - xg).astype(BF16)
        pooled.append(_dot(yg, poolw_ref[g]))
    pbuf_ref[0:POOL_HALO, :] = pbuf_ref[ts:ts + POOL_HALO, :]
    y_pool = jnp.concatenate(pooled, axis=1) * pscale_ref[...]
    y_pool = _dot(y_pool.astype(BF16), wpool_ref[...])

    y_att = _dot(o_ref[...], watt_ref[...])
    merged = None
    for bidx, y in enumerate((y_conv, y_att, y_pool)):
        gate = gt_ref[:, bidx * d:(bidx + 1) * d].astype(F32)
        merged = gate * y if merged is None else merged + gate * y
    out_ref[...] = h_ref[...] + _dot(merged.astype(BF16), wo_ref[...])


def _mixer_tail(h, hc, p, gt, o, weights, ts):
    b, s, d = h.shape
    cw = hc.shape[2]
    pw = p.shape[2]

    def tile(x):
        return pl.BlockSpec((None, ts, x.shape[2]), lambda bi, si: (bi, si, 0))

    return pl.pallas_call(
        _tail_kernel,
        out_shape=jax.ShapeDtypeStruct((b, s, d), F32),
        grid=(b, s // ts),
        in_specs=[tile(h), tile(hc), tile(p), tile(gt), tile(o)] + [_const_spec(w.shape) for w in weights],
        out_specs=tile(h),
        scratch_shapes=[pltpu.VMEM((CONV_HALO + ts, cw), F32),
                        pltpu.VMEM((POOL_HALO + ts, pw), F32),
                        pltpu.VMEM((ts, cw), F32)],
        compiler_params=pltpu.CompilerParams(dimension_semantics=("arbitrary", "arbitrary"),
                                             vmem_limit_bytes=VMEM_LIMIT),
        name="mixer_tail",
    )(h, hc, p, gt, o, *weights)


def _mlp_kernel(x_ref, g_ref, w1_ref, w2_ref, out_ref):
    dff = w1_ref.shape[1]
    rows_per = x_ref.shape[0] // ROW_SLABS
    fc = 1024

    def normed(r0):
        x = x_ref[r0:r0 + rows_per, :]
        ms = jnp.mean(x * x, axis=-1, keepdims=True)
        return ((x * lax.rsqrt(ms + EPS)) * g_ref[...]).astype(BF16)

    xn = normed(0)
    for sidx in range(ROW_SLABS):
        r0 = sidx * rows_per
        acc = x_ref[r0:r0 + rows_per, :]
        xn_next = None
        for ci, c in enumerate(range(0, dff, fc)):
            ff = jnp.maximum(_dot(xn, w1_ref[:, c:c + fc]), 0.0)
            acc = acc + _dot((ff * ff).astype(BF16), w2_ref[c:c + fc, :])
            if ci == (dff // fc) // 2 and sidx + 1 < ROW_SLABS:
                xn_next = normed(r0 + rows_per)
        out_ref[r0:r0 + rows_per, :] = acc
        xn = xn_next


def _mlp(h, g, w1, w2, tm):
    t, d = h.shape
    return pl.pallas_call(
        _mlp_kernel,
        out_shape=jax.ShapeDtypeStruct((t, d), F32),
        grid=(t // tm,),
        in_specs=[pl.BlockSpec((tm, d), lambda i: (i, 0)),
                  _const_spec(g.shape), _const_spec(w1.shape), _const_spec(w2.shape)],
        out_specs=pl.BlockSpec((tm, d), lambda i: (i, 0)),
        compiler_params=pltpu.CompilerParams(dimension_semantics=("arbitrary",),
                                             vmem_limit_bytes=VMEM_LIMIT),
        name="relu2_mlp",
    )(h, g, w1, w2)


def _pick_tile(n, target):
    t = min(n, target)
    while n % t:
        t //= 2
    return t


def kernel(x, mix_norm_g, w_in, gate_b, conv_dw, conv_dw_b, conv_ln_g, conv_ln_b, w_conv_out,
           q_norm_g, k_norm_g, w_att_out, pool_w, pool_scale, w_pool_out, w_o,
           mlp_norm_g, w_mlp_in, w_mlp_out):
    b, s, d = x.shape
    depth = w_in.shape[0]
    aw = w_att_out.shape[1]
    cw = w_conv_out.shape[1]
    pw = w_pool_out.shape[1]
    widths = (aw, cw, pw, gate_b.shape[1] * d)
    assert s % ATT_BLK == 0 and aw % LANES == 0 and LANES == 2 * HEAD_DIM
    tm = _pick_tile(b * s, TOKEN_TILE)
    assert tm % (ROW_SLABS * SUBLANES) == 0
    ts = _pick_tile(s, 512)
    tq = _pick_tile(s, ATT_STEP_BLOCKS * ATT_BLK)

    lane = jnp.arange(aw)
    gsum = (lane[:, None] // HEAD_DIM == lane[None, :] // HEAD_DIM).astype(BF16)
    r = jnp.arange(2 * ATT_BLK)
    key = r % ATT_BLK
    u2 = jnp.where(r[None, :] < ATT_BLK, key[:, None] > r[None, :], True).astype(BF16)

    h = x
    for l in range(depth):
        qg = jnp.tile(q_norm_g[l], aw // HEAD_DIM)[None, :]
        kg = jnp.tile(k_norm_g[l], aw // HEAD_DIM)[None, :]
        q, k, v, hc, p, gt = _inproj(h.reshape(b * s, d), mix_norm_g[l][None, :], w_in[l].astype(BF16),
                                     qg, kg, gsum, gate_b[l].reshape(1, -1), widths, tm)
        o = _attention(q.reshape(b, s, aw), k.reshape(b, s, aw), v.reshape(b, s, aw), u2, tq)
        weights = (conv_dw[l], conv_dw_b[l][None, :], conv_ln_g[l][None, :], conv_ln_b[l][None, :],
                   w_conv_out[l].astype(BF16), pool_w[l].astype(BF16), pool_scale[l][None, :],
                   w_pool_out[l].astype(BF16), w_att_out[l].astype(BF16), w_o[l].astype(BF16))
        h = _mixer_tail(h, hc.reshape(b, s, cw), p.reshape(b, s, pw), gt.reshape(b, s, -1), o,
                        weights, ts)
        h = _mlp(h.reshape(b * s, d), mlp_norm_g[l][None, :], w_mlp_in[l].astype(BF16),
                 w_mlp_out[l].astype(BF16), tm).reshape(b, s, d)
    return h
```

```python
import functools
import math

import jax
import jax.numpy as jnp
from jax import lax
from jax.experimental import pallas as pl
from jax.experimental.pallas import tpu as pltpu

F32 = jnp.float32
BF16 = jnp.bfloat16

EPS = 1e-6
CONV_K = 31
HEAD_DIM = 64
LANES = 128
SUBLANES = 8
POOL_WINDOWS = (2, 4, 8, 16)
CONV_HALO = 32
POOL_HALO = 16
ATT_BLK = LANES
ATT_WINDOW = 3
ATT_STEP_BLOCKS = 16
TOKEN_TILE = 1024
ROW_SLABS = 4
TAIL_SLABS = 4
LOG2E = math.log2(math.e)
STICK_EXHAUSTED_LOG2 = -105.0 * LOG2E
VMEM_LIMIT = 56 * 1024 * 1024


def _dot(a, b):
    return jnp.dot(a, b, preferred_element_type=F32)


def _split_bf16(x):
    hi = x.astype(BF16)
    lo = (x - hi.astype(F32)).astype(BF16)
    return hi, lo


def _tree_max(xs):
    while len(xs) > 1:
        xs = [jnp.maximum(xs[i], xs[i + 1]) if i + 1 < len(xs) else xs[i] for i in range(0, len(xs), 2)]
    return xs[0]


def _const_spec(shape):
    nd = len(shape)
    return pl.BlockSpec(shape, lambda *_: (0,) * nd, pipeline_mode=pl.Buffered(1))


def _inproj_kernel(x_ref, g_ref, w_ref, qg_ref, kg_ref, gsum_ref, gb_ref,
                   q_ref, k_ref, v_ref, hc_ref, p_ref, gt_ref):
    aw = q_ref.shape[1]
    cw = hc_ref.shape[1]
    pw = p_ref.shape[1]
    off_c = 3 * aw
    off_p = off_c + 2 * cw
    off_g = off_p + pw
    rows_per = x_ref.shape[0] // ROW_SLABS

    def normed(r0):
        x = x_ref[r0:r0 + rows_per, :]
        ms = jnp.mean(x * x, axis=-1, keepdims=True)
        return ((x * lax.rsqrt(ms + EPS)) * g_ref[...]).astype(BF16)

    def head_norm(t, gain):
        ss = _dot((t * t).astype(BF16), gsum_ref[...])
        return (t * lax.rsqrt(ss * (1.0 / HEAD_DIM) + EPS)) * gain

    def stages(r0):
        rs = slice(r0, r0 + rows_per)

        def emit_q(t):
            q_ref[rs, :] = (head_norm(t, qg_ref[...]) * (HEAD_DIM ** -0.5 * LOG2E)).astype(BF16)

        def emit_k(t):
            k_ref[rs, :] = head_norm(t, kg_ref[...]).astype(BF16)

        def emit_v(t):
            v_ref[rs, :] = t.astype(BF16)

        def emit_conv(t):
            hc_ref[rs, :] = (t[:, :cw] * jax.nn.sigmoid(t[:, cw:])).astype(BF16)

        def emit_p(t):
            p_ref[rs, :] = t.astype(BF16)

        def emit_gate(c):
            def emit(t):
                gt_ref[rs, c:c + 512] = jax.nn.sigmoid(t + gb_ref[:, c:c + 512]).astype(BF16)
            return emit

        out = [(0, aw, emit_q), (aw, aw, emit_k), (2 * aw, aw, emit_v), (off_c, 2 * cw, emit_conv),
               (off_p, pw, emit_p)]
        out += [(off_g + c, 512, emit_gate(c)) for c in range(0, gt_ref.shape[1], 512)]
        return out

    xn = normed(0)
    for sidx in range(ROW_SLABS):
        xn_next = None
        todo = stages(sidx * rows_per)
        for c, (start, width, emit) in enumerate(todo):
            emit(_dot(xn, w_ref[:, start:start + width]))
            if c == len(todo) // 2 and sidx + 1 < ROW_SLABS:
                xn_next = normed((sidx + 1) * rows_per)
        xn = xn_next


def _inproj(h, g, w, qg, kg, gsum, gate_b, widths, tm):
    t, d = h.shape
    aw, cw, pw, gw = widths
    outs = [(t, aw), (t, aw), (t, aw), (t, cw), (t, pw), (t, gw)]
    consts = (g, w, qg, kg, gsum, gate_b)
    return pl.pallas_call(
        _inproj_kernel,
        out_shape=[jax.ShapeDtypeStruct(s, BF16) for s in outs],
        grid=(t // tm,),
        in_specs=[pl.BlockSpec((tm, d), lambda i: (i, 0))] + [_const_spec(c.shape) for c in consts],
        out_specs=[pl.BlockSpec((tm, s[1]), lambda i: (i, 0)) for s in outs],
        compiler_params=pltpu.CompilerParams(dimension_semantics=("arbitrary",),
                                             vmem_limit_bytes=VMEM_LIMIT),
        name="inproj",
    )(h, *consts)


def _attn_kernel(q_ref, k_ref, v_ref, u2_ref, o_ref, carry_ref, acc_ref):
    qi = pl.program_id(2)
    blk = ATT_BLK
    nq = q_ref.shape[0] // blk
    head0 = lax.broadcasted_iota(jnp.int32, (blk, LANES), 1) < HEAD_DIM
    causal = (lax.broadcasted_iota(jnp.int32, (blk, 2 * blk), 1) & (blk - 1)
              < lax.broadcasted_iota(jnp.int32, (blk, 2 * blk), 0))

    def head_stack(ref, j):
        x = ref[pl.ds(pl.multiple_of(j * blk, blk), blk), :]
        zero = jnp.zeros_like(x)
        return jnp.concatenate([jnp.where(head0, x, zero), jnp.where(head0, zero, x)], axis=0)

    def logits(q_rows, k_stack):
        return lax.dot_general(q_rows, k_stack, (((1,), (1,)), ((), ())),
                               preferred_element_type=F32)

    def log_terms(z, diag):
        neg_abs = pltpu.bitcast(pltpu.bitcast(z, jnp.uint32) | jnp.uint32(0x80000000), F32)
        lb = jnp.minimum(z, 0.0) - jnp.log2(1.0 + jnp.exp2(neg_abs))
        m = lb - z
        if diag:
            m = jnp.where(causal, m, 0.0)
        return lb, m

    def block_cumsums(ms):
        rows = []
        for m in ms:
            hi, lo = _split_bf16(m)
            for h in range(2):
                rows.append(jnp.concatenate([hi[:, h * blk:(h + 1) * blk], lo[:, h * blk:(h + 1) * blk]], axis=1))
        cs = _dot(jnp.concatenate(rows, axis=0), u2_ref[...])
        out = []
        for p in range(len(ms)):
            c0 = cs[(2 * p) * blk:(2 * p + 1) * blk]
            c1 = cs[(2 * p + 1) * blk:(2 * p + 2) * blk]
            out.append((jnp.concatenate([c0[:, :blk], c1[:, :blk]], axis=1),
                        jnp.concatenate([c0[:, blk:], c1[:, blk:]], axis=1)))
        return out

    def weights(lb, excl, diag):
        a = jnp.exp2(lb + excl)
        if diag:
            a = jnp.where(causal, a, 0.0)
        return a.astype(BF16)

    def stick_max(carries):
        rows = _tree_max(list(carries))
        return jnp.max(_tree_max([rows[r:r + SUBLANES] for r in range(0, blk, SUBLANES)]))

    def sweep(first_step):
        plan = [(n, range(min(ATT_WINDOW, n + 1) if first_step else ATT_WINDOW)) for n in range(nq)]
        rel = sorted({n - back for n, backs in plan for back in backs})
        kst = {r: head_stack(k_ref, qi * nq + r) for r in rel}
        vst = {r: head_stack(v_ref, qi * nq + r) for r in rel}
        zs = [[logits(q_ref[n * blk:(n + 1) * blk, :], kst[n - back]) for back in backs] for n, backs in plan]
        terms = [[log_terms(z, back == 0) for z, back in zip(zrow, backs)] for zrow, (n, backs) in zip(zs, plan)]
        sums = [block_cumsums([m for _, m in trow]) for trow in terms]
        probs, runs = [], []
        for trow, srow, (n, backs) in zip(terms, sums, plan):
            run, prow = None, []
            for back, (lb, _), (excl, tot) in zip(backs, trow, srow):
                if run is not None:
                    excl = excl + run
                run = tot if run is None else run + tot
                prow.append(weights(lb, excl, back == 0))
            probs.append(prow)
            runs.append(run)
        for prow, run, (n, backs) in zip(probs, runs, plan):
            acc_ref[n] = _dot(jnp.concatenate(prow, axis=1),
                              jnp.concatenate([vst[n - back] for back in backs], axis=0))
            carry_ref[n] = run

    @pl.when(qi > 0)
    def _():
        sweep(False)

    @pl.when(qi == 0)
    def _():
        sweep(True)

    @pl.when(stick_max([carry_ref[n] for n in range(nq)]) >= STICK_EXHAUSTED_LOG2)
    def _():
        exhausted = [(stick_max([carry_ref[n]]) < STICK_EXHAUSTED_LOG2).astype(jnp.int32) for n in range(nq)]
        for n in range(nq):
            def cond(state):
                j, done = state
                return jnp.logical_and(j >= 0, done == 0)

            def body(state, n=n):
                j, _ = state
                lb, m = log_terms(logits(q_ref[n * blk:(n + 1) * blk, :], head_stack(k_ref, j)), False)
                (excl, tot), = block_cumsums([m])
                carry = carry_ref[n]
                acc_ref[n] += _dot(weights(lb, excl + carry, False), head_stack(v_ref, j))
                carry = carry + tot
                carry_ref[n] = carry
                return j - 1, (stick_max([carry]) < STICK_EXHAUSTED_LOG2).astype(jnp.int32)

            lax.while_loop(cond, body, (qi * nq + n - ATT_WINDOW, exhausted[n]))

    for n in range(nq):
        o_ref[n * blk:(n + 1) * blk, :] = acc_ref[n].astype(o_ref.dtype)


def _attention(q, k, v, u2, tq):
    b, s, aw = q.shape
    groups = aw // LANES
    nq = tq // ATT_BLK
    assert nq >= ATT_WINDOW - 1
    return pl.pallas_call(
        _attn_kernel,
        out_shape=jax.ShapeDtypeStruct((b, s, aw), BF16),
        grid=(b, groups, s // tq),
        in_specs=[pl.BlockSpec((None, tq, LANES), lambda bi, gi, i: (bi, i, gi)),
                  pl.BlockSpec((None, s, LANES), lambda bi, gi, i: (bi, 0, gi)),
                  pl.BlockSpec((None, s, LANES), lambda bi, gi, i: (bi, 0, gi)),
                  _const_spec(u2.shape)],
        out_specs=pl.BlockSpec((None, tq, LANES), lambda bi, gi, i: (bi, i, gi)),
        scratch_shapes=[pltpu.VMEM((nq, ATT_BLK, 2 * ATT_BLK), F32),
                        pltpu.VMEM((nq, ATT_BLK, LANES), F32)],
        compiler_params=pltpu.CompilerParams(
            dimension_semantics=("arbitrary", "arbitrary", "arbitrary"),
            vmem_limit_bytes=VMEM_LIMIT),
        name="stickbreak_attn",
    )(q, k, v, u2)


def _run_interleaved(primary, secondary):
    done = 0
    for i, thunk in enumerate(primary):
        thunk()
        while done < len(secondary) and (done + 1) * len(primary) <= (i + 1) * len(secondary):
            secondary[done]()
            done += 1
    for thunk in secondary[done:]:
        thunk()


def _conv_chunk(hbuf_ref, dw_ref, conv_ref, r0, c0, rc):
    acc = None
    for b in range(SUBLANES):
        part = None
        for a in range((CONV_K - 1 - b) // SUBLANES + 1):
            tap = CONV_K - 1 - (SUBLANES * a + b)
            start = CONV_HALO - SUBLANES + r0 - SUBLANES * a
            term = dw_ref[tap:tap + 1, c0:c0 + LANES] * hbuf_ref[start:start + rc + SUBLANES, c0:c0 + LANES]
            part = term if part is None else part + term
        if b:
            part = pltpu.roll(part, b, axis=0)
        part = part[SUBLANES:, :]
        acc = part if acc is None else acc + part
    conv_ref[r0:r0 + rc, c0:c0 + LANES] = acc


def _tail_kernel(h_ref, hc_ref, p_ref, gt_ref, o_ref,
                 dw_ref, dwb_ref, lng_ref, lnb_ref, wconv_ref,
                 poolw_ref, pscale_ref, wpool_ref, watt_ref, wo_ref,
                 out_ref, hbuf_ref, pbuf_ref, conv_ref):
    si = pl.program_id(1)
    ts, d = h_ref.shape
    cw = hbuf_ref.shape[1]
    pw = pbuf_ref.shape[1]
    rows = ts // TAIL_SLABS

    @pl.when(si == 0)
    def _():
        hbuf_ref[0:CONV_HALO, :] = jnp.zeros((CONV_HALO, cw), F32)
        pbuf_ref[0:POOL_HALO, :] = jnp.zeros((POOL_HALO, pw), F32)

    hbuf_ref[CONV_HALO:CONV_HALO + ts, :] = hc_ref[...].astype(F32)
    pbuf_ref[POOL_HALO:POOL_HALO + ts, :] = p_ref[...].astype(F32)

    def conv_thunks(k):
        rc = 64
        return [functools.partial(_conv_chunk, hbuf_ref, dw_ref, conv_ref, r0, c0, rc)
                for r0 in range(k * rows, (k + 1) * rows, rc) for c0 in range(0, cw, LANES)]

    def slab_thunks(k):
        rs = slice(k * rows, (k + 1) * rows)
        y = {"pooled": []}

        def conv_act():
            conv = conv_ref[rs, :] + dwb_ref[...]
            mu = jnp.mean(conv, axis=-1, keepdims=True)
            xc = conv - mu
            ln = (xc * lax.rsqrt(jnp.mean(xc * xc, axis=-1, keepdims=True) + EPS)) * lng_ref[...] + lnb_ref[...]
            y["act"] = (ln * jax.nn.sigmoid(ln)).astype(BF16)

        def conv_out():
            y["conv"] = _dot(y["act"], wconv_ref[...])

        def pool_group(g, w):
            def step():
                c0 = g * LANES
                tpos = si * ts + k * rows + lax.broadcasted_iota(jnp.int32, (rows, LANES), 0)
                ext = pbuf_ref[k * rows:k * rows + POOL_HALO + rows, c0:c0 + LANES]
                tot, span = ext, 1
                while span < w:
                    tot = tot + pltpu.roll(tot, span, axis=0)
                    span *= 2
                count = jnp.minimum(tpos + 1, w).astype(F32)
                yg = (tot[POOL_HALO:, :] / count - ext[POOL_HALO:, :]).astype(BF16)
                y["pooled"].append(_dot(yg, poolw_ref[g]))
            return step

        def pool_out():
            yp = jnp.concatenate(y["pooled"], axis=1) * pscale_ref[...]
            y["pool"] = _dot(yp.astype(BF16), wpool_ref[...])

        def att_out():
            y["att"] = _dot(o_ref[rs, :], watt_ref[...])

        def merge():
            merged = None
            for bidx, name in enumerate(("conv", "att", "pool")):
                gate = gt_ref[rs, bidx * d:(bidx + 1) * d].astype(F32)
                merged = gate * y[name] if merged is None else merged + gate * y[name]
            y["merged"] = merged.astype(BF16)

        def residual(c0, width):
            def step():
                out_ref[rs, c0:c0 + width] = h_ref[rs, c0:c0 + width] + _dot(y["merged"], wo_ref[:, c0:c0 + width])
            return step

        steps = [att_out, conv_act, conv_out]
        steps += [pool_group(g, w) for g, w in enumerate(POOL_WINDOWS)]
        steps += [pool_out, merge]
        steps += [residual(c0, 256) for c0 in range(0, d, 256)]
        return steps

    for thunk in conv_thunks(0):
        thunk()
    for k in range(TAIL_SLABS):
        _run_interleaved(conv_thunks(k + 1) if k + 1 < TAIL_SLABS else [], slab_thunks(k))

    hbuf_ref[0:CONV_HALO, :] = hbuf_ref[ts:ts + CONV_HALO, :]
    pbuf_ref[0:POOL_HALO, :] = pbuf_ref[ts:ts + POOL_HALO, :]


def _mixer_tail(h, hc, p, gt, o, weights, ts):
    b, s, d = h.shape
    cw = hc.shape[2]
    pw = p.shape[2]

    def tile(x):
        return pl.BlockSpec((None, ts, x.shape[2]), lambda bi, si: (bi, si, 0))

    return pl.pallas_call(
        _tail_kernel,
        out_shape=jax.ShapeDtypeStruct((b, s, d), F32),
        grid=(b, s // ts),
        in_specs=[tile(h), tile(hc), tile(p), tile(gt), tile(o)] + [_const_spec(w.shape) for w in weights],
        out_specs=tile(h),
        scratch_shapes=[pltpu.VMEM((CONV_HALO + ts, cw), F32),
                        pltpu.VMEM((POOL_HALO + ts, pw), F32),
                        pltpu.VMEM((ts, cw), F32)],
        compiler_params=pltpu.CompilerParams(dimension_semantics=("arbitrary", "arbitrary"),
                                             vmem_limit_bytes=VMEM_LIMIT),
        name="mixer_tail",
    )(h, hc, p, gt, o, *weights)


def _mlp_kernel(x_ref, g_ref, w1_ref, w2_ref, out_ref):
    dff = w1_ref.shape[1]
    rows_per = x_ref.shape[0] // ROW_SLABS
    fc = 1024

    def normed(r0):
        x = x_ref[r0:r0 + rows_per, :]
        ms = jnp.mean(x * x, axis=-1, keepdims=True)
        return ((x * lax.rsqrt(ms + EPS)) * g_ref[...]).astype(BF16)

    xn = normed(0)
    for sidx in range(ROW_SLABS):
        r0 = sidx * rows_per
        acc = x_ref[r0:r0 + rows_per, :]
        xn_next = None
        for ci, c in enumerate(range(0, dff, fc)):
            ff = jnp.maximum(_dot(xn, w1_ref[:, c:c + fc]), 0.0)
            acc = acc + _dot((ff * ff).astype(BF16), w2_ref[c:c + fc, :])
            if ci == (dff // fc) // 2 and sidx + 1 < ROW_SLABS:
                xn_next = normed(r0 + rows_per)
        out_ref[r0:r0 + rows_per, :] = acc
        xn = xn_next


def _mlp(h, g, w1, w2, tm):
    t, d = h.shape
    return pl.pallas_call(
        _mlp_kernel,
        out_shape=jax.ShapeDtypeStruct((t, d), F32),
        grid=(t // tm,),
        in_specs=[pl.BlockSpec((tm, d), lambda i: (i, 0)),
                  _const_spec(g.shape), _const_spec(w1.shape), _const_spec(w2.shape)],
        out_specs=pl.BlockSpec((tm, d), lambda i: (i, 0)),
        compiler_params=pltpu.CompilerParams(dimension_semantics=("arbitrary",),
                                             vmem_limit_bytes=VMEM_LIMIT),
        name="relu2_mlp",
    )(h, g, w1, w2)


def _cumsum_matrix():
    key = (jnp.arange(2 * ATT_BLK) % ATT_BLK)[:, None]
    col = jnp.arange(2 * ATT_BLK)[None, :]
    return jnp.where(col < ATT_BLK, key > col, True).astype(BF16)


def _pick_tile(n, target):
    t = min(n, target)
    while n % t:
        t //= 2
    return t


def kernel(x, mix_norm_g, w_in, gate_b, conv_dw, conv_dw_b, conv_ln_g, conv_ln_b, w_conv_out,
           q_norm_g, k_norm_g, w_att_out, pool_w, pool_scale, w_pool_out, w_o,
           mlp_norm_g, w_mlp_in, w_mlp_out):
    b, s, d = x.shape
    depth = w_in.shape[0]
    aw = w_att_out.shape[1]
    cw = w_conv_out.shape[1]
    pw = w_pool_out.shape[1]
    widths = (aw, cw, pw, gate_b.shape[1] * d)
    assert s % ATT_BLK == 0 and aw % LANES == 0 and LANES == 2 * HEAD_DIM
    tm = _pick_tile(b * s, TOKEN_TILE)
    assert tm % (ROW_SLABS * SUBLANES) == 0
    ts = _pick_tile(s, TOKEN_TILE)
    assert ts % (TAIL_SLABS * 64) == 0
    tq = _pick_tile(s, ATT_STEP_BLOCKS * ATT_BLK)

    lane = jnp.arange(aw)
    gsum = (lane[:, None] // HEAD_DIM == lane[None, :] // HEAD_DIM).astype(BF16)
    u2 = _cumsum_matrix()

    h = x
    for l in range(depth):
        qg = jnp.tile(q_norm_g[l], aw // HEAD_DIM)[None, :]
        kg = jnp.tile(k_norm_g[l], aw // HEAD_DIM)[None, :]
        q, k, v, hc, p, gt = _inproj(h.reshape(b * s, d), mix_norm_g[l][None, :], w_in[l].astype(BF16),
                                     qg, kg, gsum, gate_b[l].reshape(1, -1), widths, tm)
        o = _attention(q.reshape(b, s, aw), k.reshape(b, s, aw), v.reshape(b, s, aw), u2, tq)
        weights = (conv_dw[l], conv_dw_b[l][None, :], conv_ln_g[l][None, :], conv_ln_b[l][None, :],
                   w_conv_out[l].astype(BF16), pool_w[l].astype(BF16), pool_scale[l][None, :],
                   w_pool_out[l].astype(BF16), w_att_out[l].astype(BF16), w_o[l].astype(BF16))
        h = _mixer_tail(h, hc.reshape(b, s, cw), p.reshape(b, s, pw), gt.reshape(b, s, -1), o,
                        weights, ts)
        h = _mlp(h.reshape(b * s, d), mlp_norm_g[l][None, :], w_mlp_in[l].astype(BF16),
                 w_mlp_out[l].astype(BF16), tm).reshape(b, s, d)
    return h
```

```python
import math

import jax
import jax.numpy as jnp
from jax import lax
from jax.experimental import pallas as pl
from jax.experimental.pallas import tpu as pltpu

F32 = jnp.float32
BF16 = jnp.bfloat16

EPS = 1e-6
CONV_K = 31
HEAD_DIM = 64
LANES = 128
SUBLANES = 8
POOL_WINDOWS = (2, 4, 8, 16)
CONV_HALO = 32
POOL_HALO = 16
ATT_BLK = LANES
ATT_WINDOW = 3
ATT_STEP_BLOCKS = 16
TOKEN_TILE = 1024
ROW_SLABS = 4
LOG2E = math.log2(math.e)
STICK_EXHAUSTED_LOG2 = -105.0 * LOG2E
VMEM_LIMIT = 56 * 1024 * 1024


def _dot(a, b):
    return jnp.dot(a, b, preferred_element_type=F32)


def _split_bf16(x):
    hi = x.astype(BF16)
    lo = (x - hi.astype(F32)).astype(BF16)
    return hi, lo


def _tree_max(xs):
    while len(xs) > 1:
        xs = [jnp.maximum(xs[i], xs[i + 1]) if i + 1 < len(xs) else xs[i] for i in range(0, len(xs), 2)]
    return xs[0]


def _const_spec(shape):
    nd = len(shape)
    return pl.BlockSpec(shape, lambda *_: (0,) * nd, pipeline_mode=pl.Buffered(1))


def _inproj_kernel(x_ref, g_ref, w_ref, qg_ref, kg_ref, gsum_ref, gb_ref,
                   q_ref, k_ref, v_ref, hc_ref, p_ref, gt_ref):
    aw = q_ref.shape[1]
    cw = hc_ref.shape[1]
    pw = p_ref.shape[1]
    off_c = 3 * aw
    off_p = off_c + 2 * cw
    off_g = off_p + pw
    rows_per = x_ref.shape[0] // ROW_SLABS

    def normed(r0):
        x = x_ref[r0:r0 + rows_per, :]
        ms = jnp.mean(x * x, axis=-1, keepdims=True)
        return ((x * lax.rsqrt(ms + EPS)) * g_ref[...]).astype(BF16)

    def head_norm(t, gain):
        ss = _dot((t * t).astype(BF16), gsum_ref[...])
        return (t * lax.rsqrt(ss * (1.0 / HEAD_DIM) + EPS)) * gain

    def stages(r0):
        rs = slice(r0, r0 + rows_per)

        def emit_q(t):
            q_ref[rs, :] = (head_norm(t, qg_ref[...]) * (HEAD_DIM ** -0.5 * LOG2E)).astype(BF16)

        def emit_k(t):
            k_ref[rs, :] = head_norm(t, kg_ref[...]).astype(BF16)

        def emit_v(t):
            v_ref[rs, :] = t.astype(BF16)

        def emit_conv(t):
            hc_ref[rs, :] = (t[:, :cw] * jax.nn.sigmoid(t[:, cw:])).astype(BF16)

        def emit_p(t):
            p_ref[rs, :] = t.astype(BF16)

        def emit_gate(c):
            def emit(t):
                gt_ref[rs, c:c + 512] = jax.nn.sigmoid(t + gb_ref[:, c:c + 512]).astype(BF16)
            return emit

        out = [(0, aw, emit_q), (aw, aw, emit_k), (2 * aw, aw, emit_v), (off_c, 2 * cw, emit_conv),
               (off_p, pw, emit_p)]
        out += [(off_g + c, 512, emit_gate(c)) for c in range(0, gt_ref.shape[1], 512)]
        return out

    xn = normed(0)
    for sidx in range(ROW_SLABS):
        xn_next = None
        todo = stages(sidx * rows_per)
        for c, (start, width, emit) in enumerate(todo):
            emit(_dot(xn, w_ref[:, start:start + width]))
            if c == len(todo) // 2 and sidx + 1 < ROW_SLABS:
                xn_next = normed((sidx + 1) * rows_per)
        xn = xn_next


def _inproj(h, g, w, qg, kg, gsum, gate_b, widths, tm):
    t, d = h.shape
    aw, cw, pw, gw = widths
    outs = [(t, aw), (t, aw), (t, aw), (t, cw), (t, pw), (t, gw)]
    consts = (g, w, qg, kg, gsum, gate_b)
    return pl.pallas_call(
        _inproj_kernel,
        out_shape=[jax.ShapeDtypeStruct(s, BF16) for s in outs],
        grid=(t // tm,),
        in_specs=[pl.BlockSpec((tm, d), lambda i: (i, 0))] + [_const_spec(c.shape) for c in consts],
        out_specs=[pl.BlockSpec((tm, s[1]), lambda i: (i, 0)) for s in outs],
        compiler_params=pltpu.CompilerParams(dimension_semantics=("arbitrary",),
                                             vmem_limit_bytes=VMEM_LIMIT),
        name="inproj",
    )(h, *consts)


def _attn_kernel(q_ref, k_ref, v_ref, u2_ref, o_ref, carry_ref, acc_ref):
    qi = pl.program_id(2)
    blk = ATT_BLK
    nq = q_ref.shape[0] // blk
    head0 = lax.broadcasted_iota(jnp.int32, (blk, LANES), 1) < HEAD_DIM
    causal = (lax.broadcasted_iota(jnp.int32, (blk, 2 * blk), 1) & (blk - 1)
              < lax.broadcasted_iota(jnp.int32, (blk, 2 * blk), 0))

    def head_stack(ref, j):
        x = ref[pl.ds(pl.multiple_of(j * blk, blk), blk), :]
        zero = jnp.zeros_like(x)
        return jnp.concatenate([jnp.where(head0, x, zero), jnp.where(head0, zero, x)], axis=0)

    def logits(q_rows, k_stack):
        return lax.dot_general(q_rows, k_stack, (((1,), (1,)), ((), ())),
                               preferred_element_type=F32)

    def log_terms(z, diag):
        neg_abs = pltpu.bitcast(pltpu.bitcast(z, jnp.uint32) | jnp.uint32(0x80000000), F32)
        lb = jnp.minimum(z, 0.0) - jnp.log2(1.0 + jnp.exp2(neg_abs))
        m = lb - z
        if diag:
            m = jnp.where(causal, m, 0.0)
        return lb, m

    def block_cumsums(ms):
        rows = []
        for m in ms:
            hi, lo = _split_bf16(m)
            for h in range(2):
                rows.append(jnp.concatenate([hi[:, h * blk:(h + 1) * blk], lo[:, h * blk:(h + 1) * blk]], axis=1))
        cs = _dot(jnp.concatenate(rows, axis=0), u2_ref[...])
        out = []
        for p in range(len(ms)):
            c0 = cs[(2 * p) * blk:(2 * p + 1) * blk]
            c1 = cs[(2 * p + 1) * blk:(2 * p + 2) * blk]
            out.append((jnp.concatenate([c0[:, :blk], c1[:, :blk]], axis=1),
                        jnp.concatenate([c0[:, blk:], c1[:, blk:]], axis=1)))
        return out

    def weights(lb, excl, diag):
        a = jnp.exp2(lb + excl)
        if diag:
            a = jnp.where(causal, a, 0.0)
        return a.astype(BF16)

    def stick_max(carries):
        rows = _tree_max(list(carries))
        return jnp.max(_tree_max([rows[r:r + SUBLANES] for r in range(0, blk, SUBLANES)]))

    def sweep(first_step):
        plan = [(n, range(min(ATT_WINDOW, n + 1) if first_step else ATT_WINDOW)) for n in range(nq)]
        rel = sorted({n - back for n, backs in plan for back in backs})
        kst = {r: head_stack(k_ref, qi * nq + r) for r in rel}
        vst = {r: head_stack(v_ref, qi * nq + r) for r in rel}
        zs = [[logits(q_ref[n * blk:(n + 1) * blk, :], kst[n - back]) for back in backs] for n, backs in plan]
        terms = [[log_terms(z, back == 0) for z, back in zip(zrow, backs)] for zrow, (n, backs) in zip(zs, plan)]
        sums = [block_cumsums([m for _, m in trow]) for trow in terms]
        probs, runs = [], []
        for trow, srow, (n, backs) in zip(terms, sums, plan):
            run, prow = None, []
            for back, (lb, _), (excl, tot) in zip(backs, trow, srow):
                if run is not None:
                    excl = excl + run
                run = tot if run is None else run + tot
                prow.append(weights(lb, excl, back == 0))
            probs.append(prow)
            runs.append(run)
        for prow, run, (n, backs) in zip(probs, runs, plan):
            acc_ref[n] = _dot(jnp.concatenate(prow, axis=1),
                              jnp.concatenate([vst[n - back] for back in backs], axis=0))
            carry_ref[n] = run

    @pl.when(qi > 0)
    def _():
        sweep(False)

    @pl.when(qi == 0)
    def _():
        sweep(True)

    @pl.when(stick_max([carry_ref[n] for n in range(nq)]) >= STICK_EXHAUSTED_LOG2)
    def _():
        exhausted = [(stick_max([carry_ref[n]]) < STICK_EXHAUSTED_LOG2).astype(jnp.int32) for n in range(nq)]
        for n in range(nq):
            def cond(state):
                j, done = state
                return jnp.logical_and(j >= 0, done == 0)

            def body(state, n=n):
                j, _ = state
                lb, m = log_terms(logits(q_ref[n * blk:(n + 1) * blk, :], head_stack(k_ref, j)), False)
                (excl, tot), = block_cumsums([m])
                carry = carry_ref[n]
                acc_ref[n] += _dot(weights(lb, excl + carry, False), head_stack(v_ref, j))
                carry = carry + tot
                carry_ref[n] = carry
                return j - 1, (stick_max([carry]) < STICK_EXHAUSTED_LOG2).astype(jnp.int32)

            lax.while_loop(cond, body, (qi * nq + n - ATT_WINDOW, exhausted[n]))

    for n in range(nq):
        o_ref[n * blk:(n + 1) * blk, :] = acc_ref[n].astype(o_ref.dtype)


def _attention(q, k, v, u2, tq):
    b, s, aw = q.shape
    groups = aw // LANES
    nq = tq // ATT_BLK
    assert nq >= ATT_WINDOW - 1
    return pl.pallas_call(
        _attn_kernel,
        out_shape=jax.ShapeDtypeStruct((b, s, aw), BF16),
        grid=(b, groups, s // tq),
        in_specs=[pl.BlockSpec((None, tq, LANES), lambda bi, gi, i: (bi, i, gi)),
                  pl.BlockSpec((None, s, LANES), lambda bi, gi, i: (bi, 0, gi)),
                  pl.BlockSpec((None, s, LANES), lambda bi, gi, i: (bi, 0, gi)),
                  _const_spec(u2.shape)],
        out_specs=pl.BlockSpec((None, tq, LANES), lambda bi, gi, i: (bi, i, gi)),
        scratch_shapes=[pltpu.VMEM((nq, ATT_BLK, 2 * ATT_BLK), F32),
                        pltpu.VMEM((nq, ATT_BLK, LANES), F32)],
        compiler_params=pltpu.CompilerParams(
            dimension_semantics=("arbitrary", "arbitrary", "arbitrary"),
            vmem_limit_bytes=VMEM_LIMIT),
        name="stickbreak_attn",
    )(q, k, v, u2)


def _conv_chunk(hbuf_ref, dw_ref, conv_ref, r0, c0, rc):
    acc = None
    for b in range(SUBLANES):
        part = None
        for a in range((CONV_K - 1 - b) // SUBLANES + 1):
            tap = CONV_K - 1 - (SUBLANES * a + b)
            start = CONV_HALO - SUBLANES + r0 - SUBLANES * a
            term = dw_ref[tap:tap + 1, c0:c0 + LANES] * hbuf_ref[start:start + rc + SUBLANES, c0:c0 + LANES]
            part = term if part is None else part + term
        if b:
            part = pltpu.roll(part, b, axis=0)
        part = part[SUBLANES:, :]
        acc = part if acc is None else acc + part
    conv_ref[r0:r0 + rc, c0:c0 + LANES] = acc


def _tail_kernel(h_ref, hc_ref, p_ref, gt_ref, o_ref,
                 dw_ref, dwb_ref, lng_ref, lnb_ref, wconv_ref,
                 poolw_ref, pscale_ref, wpool_ref, watt_ref, wo_ref,
                 out_ref, hbuf_ref, pbuf_ref, conv_ref):
    si = pl.program_id(1)
    ts, d = h_ref.shape
    cw = hbuf_ref.shape[1]
    pw = pbuf_ref.shape[1]
    rows = ts // ROW_SLABS

    @pl.when(si == 0)
    def _():
        hbuf_ref[0:CONV_HALO, :] = jnp.zeros((CONV_HALO, cw), F32)
        pbuf_ref[0:POOL_HALO, :] = jnp.zeros((POOL_HALO, pw), F32)

    hbuf_ref[CONV_HALO:CONV_HALO + ts, :] = hc_ref[...].astype(F32)
    pbuf_ref[POOL_HALO:POOL_HALO + ts, :] = p_ref[...].astype(F32)

    def slab_steps(k):
        rs = slice(k * rows, (k + 1) * rows)
        y = {"pooled": []}

        def conv_act():
            conv = conv_ref[rs, :] + dwb_ref[...]
            mu = jnp.mean(conv, axis=-1, keepdims=True)
            xc = conv - mu
            ln = (xc * lax.rsqrt(jnp.mean(xc * xc, axis=-1, keepdims=True) + EPS)) * lng_ref[...] + lnb_ref[...]
            y["act"] = (ln * jax.nn.sigmoid(ln)).astype(BF16)

        def conv_out():
            y["conv"] = _dot(y["act"], wconv_ref[...])

        def pool_group(g, w):
            def step():
                c0 = g * LANES
                tpos = si * ts + k * rows + lax.broadcasted_iota(jnp.int32, (rows, LANES), 0)
                ext = pbuf_ref[k * rows:k * rows + POOL_HALO + rows, c0:c0 + LANES]
                tot, span = ext, 1
                while span < w:
                    tot = tot + pltpu.roll(tot, span, axis=0)
                    span *= 2
                count = jnp.minimum(tpos + 1, w).astype(F32)
                yg = (tot[POOL_HALO:, :] / count - ext[POOL_HALO:, :]).astype(BF16)
                y["pooled"].append(_dot(yg, poolw_ref[g]))
            return step

        def pool_out():
            yp = jnp.concatenate(y["pooled"], axis=1) * pscale_ref[...]
            y["pool"] = _dot(yp.astype(BF16), wpool_ref[...])

        def att_out():
            y["att"] = _dot(o_ref[rs, :], watt_ref[...])

        def merge():
            merged = None
            for bidx, name in enumerate(("conv", "att", "pool")):
                gate = gt_ref[rs, bidx * d:(bidx + 1) * d].astype(F32)
                merged = gate * y[name] if merged is None else merged + gate * y[name]
            y["merged"] = merged.astype(BF16)

        def residual(c0, width):
            def step():
                out_ref[rs, c0:c0 + width] = h_ref[rs, c0:c0 + width] + _dot(y["merged"], wo_ref[:, c0:c0 + width])
            return step

        steps = [att_out, conv_act, conv_out]
        steps += [pool_group(g, w) for g, w in enumerate(POOL_WINDOWS)]
        steps += [pool_out, merge]
        steps += [residual(c0, 256) for c0 in range(0, d, 256)]
        return steps

    for k in range(ROW_SLABS):
        for c0 in range(0, cw, LANES):
            _conv_chunk(hbuf_ref, dw_ref, conv_ref, k * rows, c0, rows)
        for step in slab_steps(k):
            step()

    hbuf_ref[0:CONV_HALO, :] = hbuf_ref[ts:ts + CONV_HALO, :]
    pbuf_ref[0:POOL_HALO, :] = pbuf_ref[ts:ts + POOL_HALO, :]


def _mixer_tail(h, hc, p, gt, o, weights, ts):
    b, s, d = h.shape
    cw = hc.shape[2]
    pw = p.shape[2]

    def tile(x):
        return pl.BlockSpec((None, ts, x.shape[2]), lambda bi, si: (bi, si, 0))

    return pl.pallas_call(
        _tail_kernel,
        out_shape=jax.ShapeDtypeStruct((b, s, d), F32),
        grid=(b, s // ts),
        in_specs=[tile(h), tile(hc), tile(p), tile(gt), tile(o)] + [_const_spec(w.shape) for w in weights],
        out_specs=tile(h),
        scratch_shapes=[pltpu.VMEM((CONV_HALO + ts, cw), F32),
                        pltpu.VMEM((POOL_HALO + ts, pw), F32),
                        pltpu.VMEM((ts, cw), F32)],
        compiler_params=pltpu.CompilerParams(dimension_semantics=("arbitrary", "arbitrary"),
                                             vmem_limit_bytes=VMEM_LIMIT),
        name="mixer_tail",
    )(h, hc, p, gt, o, *weights)


def _mlp_kernel(x_ref, g_ref, w1_ref, w2_ref, out_ref):
    dff = w1_ref.shape[1]
    rows_per = x_ref.shape[0] // ROW_SLABS
    fc = 1024

    def normed(r0):
        x = x_ref[r0:r0 + rows_per, :]
        ms = jnp.mean(x * x, axis=-1, keepdims=True)
        return ((x * lax.rsqrt(ms + EPS)) * g_ref[...]).astype(BF16)

    xn = normed(0)
    for sidx in range(ROW_SLABS):
        r0 = sidx * rows_per
        acc = x_ref[r0:r0 + rows_per, :]
        xn_next = None
        for ci, c in enumerate(range(0, dff, fc)):
            ff = jnp.maximum(_dot(xn, w1_ref[:, c:c + fc]), 0.0)
            acc = acc + _dot((ff * ff).astype(BF16), w2_ref[c:c + fc, :])
            if ci == (dff // fc) // 2 and sidx + 1 < ROW_SLABS:
                xn_next = normed(r0 + rows_per)
        out_ref[r0:r0 + rows_per, :] = acc
        xn = xn_next


def _mlp(h, g, w1, w2, tm):
    t, d = h.shape
    return pl.pallas_call(
        _mlp_kernel,
        out_shape=jax.ShapeDtypeStruct((t, d), F32),
        grid=(t // tm,),
        in_specs=[pl.BlockSpec((tm, d), lambda i: (i, 0)),
                  _const_spec(g.shape), _const_spec(w1.shape), _const_spec(w2.shape)],
        out_specs=pl.BlockSpec((tm, d), lambda i: (i, 0)),
        compiler_params=pltpu.CompilerParams(dimension_semantics=("arbitrary",),
                                             vmem_limit_bytes=VMEM_LIMIT),
        name="relu2_mlp",
    )(h, g, w1, w2)


def _cumsum_matrix():
    key = (jnp.arange(2 * ATT_BLK) % ATT_BLK)[:, None]
    col = jnp.arange(2 * ATT_BLK)[None, :]
    return jnp.where(col < ATT_BLK, key > col, True).astype(BF16)


def _pick_tile(n, target):
    t = min(n, target)
    while n % t:
        t //= 2
    return t


def kernel(x, mix_norm_g, w_in, gate_b, conv_dw, conv_dw_b, conv_ln_g, conv_ln_b, w_conv_out,
           q_norm_g, k_norm_g, w_att_out, pool_w, pool_scale, w_pool_out, w_o,
           mlp_norm_g, w_mlp_in, w_mlp_out):
    b, s, d = x.shape
    depth = w_in.shape[0]
    aw = w_att_out.shape[1]
    cw = w_conv_out.shape[1]
    pw = w_pool_out.shape[1]
    widths = (aw, cw, pw, gate_b.shape[1] * d)
    assert s % ATT_BLK == 0 and aw % LANES == 0 and LANES == 2 * HEAD_DIM
    tm = _pick_tile(b * s, TOKEN_TILE)
    ts = _pick_tile(s, TOKEN_TILE)
    assert tm % (ROW_SLABS * SUBLANES) == 0 and ts % (ROW_SLABS * SUBLANES) == 0
    tq = _pick_tile(s, ATT_STEP_BLOCKS * ATT_BLK)

    lane = jnp.arange(aw)
    gsum = (lane[:, None] // HEAD_DIM == lane[None, :] // HEAD_DIM).astype(BF16)
    u2 = _cumsum_matrix()

    h = x
    for l in range(depth):
        qg = jnp.tile(q_norm_g[l], aw // HEAD_DIM)[None, :]
        kg = jnp.tile(k_norm_g[l], aw // HEAD_DIM)[None, :]
        q, k, v, hc, p, gt = _inproj(h.reshape(b * s, d), mix_norm_g[l][None, :], w_in[l].astype(BF16),
                                     qg, kg, gsum, gate_b[l].reshape(1, -1), widths, tm)
        o = _attention(q.reshape(b, s, aw), k.reshape(b, s, aw), v.reshape(b, s, aw), u2, tq)
        weights = (conv_dw[l], conv_dw_b[l][None, :], conv_ln_g[l][None, :], conv_ln_b[l][None, :],
                   w_conv_out[l].astype(BF16), pool_w[l].astype(BF16), pool_scale[l][None, :],
                   w_pool_out[l].astype(BF16), w_att_out[l].astype(BF16), w_o[l].astype(BF16))
        h = _mixer_tail(h, hc.reshape(b, s, cw), p.reshape(b, s, pw), gt.reshape(b, s, -1), o,
                        weights, ts)
        h = _mlp(h.reshape(b * s, d), mlp_norm_g[l][None, :], w_mlp_in[l].astype(BF16),
                 w_mlp_out[l].astype(BF16), tm).reshape(b, s, d)
    return h
```

```python
import math

import jax
import jax.numpy as jnp
from jax import lax
from jax.experimental import pallas as pl
from jax.experimental.pallas import tpu as pltpu

F32 = jnp.float32
BF16 = jnp.bfloat16

EPS = 1e-6
CONV_K = 31
HEAD_DIM = 64
LANES = 128
SUBLANES = 8
POOL_WINDOWS = (2, 4, 8, 16)
CONV_HALO = 32
POOL_HALO = 16
ATT_BLK = LANES
ATT_WINDOW = 3
ATT_STEP_BLOCKS = 32
TOKEN_TILE = 1024
ROW_SLABS = 4
LOG2E = math.log2(math.e)
STICK_EXHAUSTED_LOG2 = -105.0 * LOG2E
VMEM_LIMIT = 56 * 1024 * 1024


def _dot(a, b):
    return jnp.dot(a, b, preferred_element_type=F32)


def _split_bf16(x):
    hi = x.astype(BF16)
    lo = (x - hi.astype(F32)).astype(BF16)
    return hi, lo


def _tree_max(xs):
    while len(xs) > 1:
        xs = [jnp.maximum(xs[i], xs[i + 1]) if i + 1 < len(xs) else xs[i] for i in range(0, len(xs), 2)]
    return xs[0]


def _const_spec(shape):
    nd = len(shape)
    return pl.BlockSpec(shape, lambda *_: (0,) * nd, pipeline_mode=pl.Buffered(1))


def _inproj_kernel(x_ref, g_ref, w_ref, qg_ref, kg_ref, gsum_ref, gb_ref,
                   q_ref, k_ref, v_ref, hc_ref, p_ref, gt_ref):
    aw = q_ref.shape[1]
    cw = hc_ref.shape[1]
    pw = p_ref.shape[1]
    off_c = 3 * aw
    off_p = off_c + 2 * cw
    off_g = off_p + pw
    rows_per = x_ref.shape[0] // ROW_SLABS

    def normed(r0):
        x = x_ref[r0:r0 + rows_per, :]
        ms = jnp.mean(x * x, axis=-1, keepdims=True)
        return ((x * lax.rsqrt(ms + EPS)) * g_ref[...]).astype(BF16)

    def head_norm(t, gain):
        ss = _dot((t * t).astype(BF16), gsum_ref[...])
        return (t * lax.rsqrt(ss * (1.0 / HEAD_DIM) + EPS)) * gain

    def stages(r0):
        rs = slice(r0, r0 + rows_per)

        def emit_q(t):
            q_ref[rs, :] = (head_norm(t, qg_ref[...]) * (HEAD_DIM ** -0.5 * LOG2E)).astype(BF16)

        def emit_k(t):
            k_ref[rs, :] = head_norm(t, kg_ref[...]).astype(BF16)

        def emit_v(t):
            v_ref[rs, :] = t.astype(BF16)

        def emit_conv(t):
            hc_ref[rs, :] = (t[:, :cw] * jax.nn.sigmoid(t[:, cw:])).astype(BF16)

        def emit_p(t):
            p_ref[rs, :] = t.astype(BF16)

        def emit_gate(c):
            def emit(t):
                gt_ref[rs, c:c + 512] = jax.nn.sigmoid(t + gb_ref[:, c:c + 512]).astype(BF16)
            return emit

        out = [(0, aw, emit_q), (aw, aw, emit_k), (2 * aw, aw, emit_v), (off_c, 2 * cw, emit_conv),
               (off_p, pw, emit_p)]
        out += [(off_g + c, 512, emit_gate(c)) for c in range(0, gt_ref.shape[1], 512)]
        return out

    xn = normed(0)
    for sidx in range(ROW_SLABS):
        xn_next = None
        todo = stages(sidx * rows_per)
        for c, (start, width, emit) in enumerate(todo):
            emit(_dot(xn, w_ref[:, start:start + width]))
            if c == len(todo) // 2 and sidx + 1 < ROW_SLABS:
                xn_next = normed((sidx + 1) * rows_per)
        xn = xn_next


def _inproj(h, g, w, qg, kg, gsum, gate_b, widths, tm):
    t, d = h.shape
    aw, cw, pw, gw = widths
    outs = [(t, aw), (t, aw), (t, aw), (t, cw), (t, pw), (t, gw)]
    consts = (g, w, qg, kg, gsum, gate_b)
    return pl.pallas_call(
        _inproj_kernel,
        out_shape=[jax.ShapeDtypeStruct(s, BF16) for s in outs],
        grid=(t // tm,),
        in_specs=[pl.BlockSpec((tm, d), lambda i: (i, 0))] + [_const_spec(c.shape) for c in consts],
        out_specs=[pl.BlockSpec((tm, s[1]), lambda i: (i, 0)) for s in outs],
        compiler_params=pltpu.CompilerParams(dimension_semantics=("arbitrary",),
                                             vmem_limit_bytes=VMEM_LIMIT),
        name="inproj",
    )(h, *consts)


def _attn_kernel(q_ref, k_ref, v_ref, u2_ref, o_ref, carry_ref, acc_ref):
    qi = pl.program_id(2)
    blk = ATT_BLK
    nq = q_ref.shape[0] // blk
    head0 = lax.broadcasted_iota(jnp.int32, (blk, LANES), 1) < HEAD_DIM
    causal = (lax.broadcasted_iota(jnp.int32, (blk, 2 * blk), 1) & (blk - 1)
              < lax.broadcasted_iota(jnp.int32, (blk, 2 * blk), 0))

    def head_stack(ref, j):
        x = ref[pl.ds(pl.multiple_of(j * blk, blk), blk), :]
        zero = jnp.zeros_like(x)
        return jnp.concatenate([jnp.where(head0, x, zero), jnp.where(head0, zero, x)], axis=0)

    def logits(q_rows, k_stack):
        return lax.dot_general(q_rows, k_stack, (((1,), (1,)), ((), ())),
                               preferred_element_type=F32)

    def log_terms(z, diag):
        neg_abs = pltpu.bitcast(pltpu.bitcast(z, jnp.uint32) | jnp.uint32(0x80000000), F32)
        lb = jnp.minimum(z, 0.0) - jnp.log2(1.0 + jnp.exp2(neg_abs))
        m = lb - z
        if diag:
            m = jnp.where(causal, m, 0.0)
        return lb, m

    def block_cumsums(ms):
        rows = []
        for m in ms:
            hi, lo = _split_bf16(m)
            for h in range(2):
                rows.append(jnp.concatenate([hi[:, h * blk:(h + 1) * blk], lo[:, h * blk:(h + 1) * blk]], axis=1))
        cs = _dot(jnp.concatenate(rows, axis=0), u2_ref[...])
        out = []
        for p in range(len(ms)):
            c0 = cs[(2 * p) * blk:(2 * p + 1) * blk]
            c1 = cs[(2 * p + 1) * blk:(2 * p + 2) * blk]
            out.append((jnp.concatenate([c0[:, :blk], c1[:, :blk]], axis=1),
                        jnp.concatenate([c0[:, blk:], c1[:, blk:]], axis=1)))
        return out

    def weights(lb, excl, diag):
        a = jnp.exp2(lb + excl)
        if diag:
            a = jnp.where(causal, a, 0.0)
        return a.astype(BF16)

    def stick_max(carries):
        rows = _tree_max(list(carries))
        return jnp.max(_tree_max([rows[r:r + SUBLANES] for r in range(0, blk, SUBLANES)]))

    def sweep(first_step):
        plan = [(n, range(min(ATT_WINDOW, n + 1) if first_step else ATT_WINDOW)) for n in range(nq)]
        rel = sorted({n - back for n, backs in plan for back in backs})
        kst = {r: head_stack(k_ref, qi * nq + r) for r in rel}
        vst = {r: head_stack(v_ref, qi * nq + r) for r in rel}
        zs = [[logits(q_ref[n * blk:(n + 1) * blk, :], kst[n - back]) for back in backs] for n, backs in plan]
        terms = [[log_terms(z, back == 0) for z, back in zip(zrow, backs)] for zrow, (n, backs) in zip(zs, plan)]
        sums = [block_cumsums([m for _, m in trow]) for trow in terms]
        probs, runs = [], []
        for trow, srow, (n, backs) in zip(terms, sums, plan):
            run, prow = None, []
            for back, (lb, _), (excl, tot) in zip(backs, trow, srow):
                if run is not None:
                    excl = excl + run
                run = tot if run is None else run + tot
                prow.append(weights(lb, excl, back == 0))
            probs.append(prow)
            runs.append(run)
        for prow, run, (n, backs) in zip(probs, runs, plan):
            acc_ref[n] = _dot(jnp.concatenate(prow, axis=1),
                              jnp.concatenate([vst[n - back] for back in backs], axis=0))
            carry_ref[n] = run

    @pl.when(qi > 0)
    def _():
        sweep(False)

    @pl.when(qi == 0)
    def _():
        sweep(True)

    @pl.when(stick_max([carry_ref[n] for n in range(nq)]) >= STICK_EXHAUSTED_LOG2)
    def _():
        exhausted = [(stick_max([carry_ref[n]]) < STICK_EXHAUSTED_LOG2).astype(jnp.int32) for n in range(nq)]
        for n in range(nq):
            def cond(state):
                j, done = state
                return jnp.logical_and(j >= 0, done == 0)

            def body(state, n=n):
                j, _ = state
                lb, m = log_terms(logits(q_ref[n * blk:(n + 1) * blk, :], head_stack(k_ref, j)), False)
                (excl, tot), = block_cumsums([m])
                carry = carry_ref[n]
                acc_ref[n] += _dot(weights(lb, excl + carry, False), head_stack(v_ref, j))
                carry = carry + tot
                carry_ref[n] = carry
                return j - 1, (stick_max([carry]) < STICK_EXHAUSTED_LOG2).astype(jnp.int32)

            lax.while_loop(cond, body, (qi * nq + n - ATT_WINDOW, exhausted[n]))

    for n in range(nq):
        o_ref[n * blk:(n + 1) * blk, :] = acc_ref[n].astype(o_ref.dtype)


def _attention(q, k, v, u2, tq):
    b, s, aw = q.shape
    groups = aw // LANES
    nq = tq // ATT_BLK
    assert nq >= ATT_WINDOW - 1
    return pl.pallas_call(
        _attn_kernel,
        out_shape=jax.ShapeDtypeStruct((b, s, aw), BF16),
        grid=(b, groups, s // tq),
        in_specs=[pl.BlockSpec((None, tq, LANES), lambda bi, gi, i: (bi, i, gi)),
                  pl.BlockSpec((None, s, LANES), lambda bi, gi, i: (bi, 0, gi)),
                  pl.BlockSpec((None, s, LANES), lambda bi, gi, i: (bi, 0, gi)),
                  _const_spec(u2.shape)],
        out_specs=pl.BlockSpec((None, tq, LANES), lambda bi, gi, i: (bi, i, gi)),
        scratch_shapes=[pltpu.VMEM((nq, ATT_BLK, 2 * ATT_BLK), F32),
                        pltpu.VMEM((nq, ATT_BLK, LANES), F32)],
        compiler_params=pltpu.CompilerParams(
            dimension_semantics=("arbitrary", "arbitrary", "arbitrary"),
            vmem_limit_bytes=VMEM_LIMIT),
        name="stickbreak_attn",
    )(q, k, v, u2)


def _conv_chunk(hbuf_ref, dw_ref, conv_ref, r0, c0, rc):
    acc = None
    for b in range(SUBLANES):
        part = None
        for a in range((CONV_K - 1 - b) // SUBLANES + 1):
            tap = CONV_K - 1 - (SUBLANES * a + b)
            start = CONV_HALO - SUBLANES + r0 - SUBLANES * a
            term = dw_ref[tap:tap + 1, c0:c0 + LANES] * hbuf_ref[start:start + rc + SUBLANES, c0:c0 + LANES]
            part = term if part is None else part + term
        if b:
            part = pltpu.roll(part, b, axis=0)
        part = part[SUBLANES:, :]
        acc = part if acc is None else acc + part
    conv_ref[r0:r0 + rc, c0:c0 + LANES] = acc


def _tail_kernel(h_ref, hc_ref, p_ref, gt_ref, o_ref,
                 dw_ref, dwb_ref, lng_ref, lnb_ref, wconv_ref,
                 poolw_ref, pscale_ref, wpool_ref, watt_ref, wo_ref,
                 out_ref, hbuf_ref, pbuf_ref, conv_ref):
    si = pl.program_id(1)
    ts, d = h_ref.shape
    cw = hbuf_ref.shape[1]
    pw = pbuf_ref.shape[1]
    rows = ts // ROW_SLABS

    @pl.when(si == 0)
    def _():
        hbuf_ref[0:CONV_HALO, :] = jnp.zeros((CONV_HALO, cw), F32)
        pbuf_ref[0:POOL_HALO, :] = jnp.zeros((POOL_HALO, pw), F32)

    hbuf_ref[CONV_HALO:CONV_HALO + ts, :] = hc_ref[...].astype(F32)
    pbuf_ref[POOL_HALO:POOL_HALO + ts, :] = p_ref[...].astype(F32)

    def slab_steps(k):
        rs = slice(k * rows, (k + 1) * rows)
        y = {"pooled": []}

        def conv_act():
            conv = conv_ref[rs, :] + dwb_ref[...]
            mu = jnp.mean(conv, axis=-1, keepdims=True)
            xc = conv - mu
            ln = (xc * lax.rsqrt(jnp.mean(xc * xc, axis=-1, keepdims=True) + EPS)) * lng_ref[...] + lnb_ref[...]
            y["act"] = (ln * jax.nn.sigmoid(ln)).astype(BF16)

        def conv_out():
            y["conv"] = _dot(y["act"], wconv_ref[...])

        def pool_group(g, w):
            def step():
                c0 = g * LANES
                tpos = si * ts + k * rows + lax.broadcasted_iota(jnp.int32, (rows, LANES), 0)
                ext = pbuf_ref[k * rows:k * rows + POOL_HALO + rows, c0:c0 + LANES]
                tot, span = ext, 1
                while span < w:
                    tot = tot + pltpu.roll(tot, span, axis=0)
                    span *= 2
                count = jnp.minimum(tpos + 1, w).astype(F32)
                yg = (tot[POOL_HALO:, :] / count - ext[POOL_HALO:, :]).astype(BF16)
                y["pooled"].append(_dot(yg, poolw_ref[g]))
            return step

        def pool_out():
            yp = jnp.concatenate(y["pooled"], axis=1) * pscale_ref[...]
            y["pool"] = _dot(yp.astype(BF16), wpool_ref[...])

        def att_out():
            y["att"] = _dot(o_ref[rs, :], watt_ref[...])

        def merge():
            merged = None
            for bidx, name in enumerate(("conv", "att", "pool")):
                gate = gt_ref[rs, bidx * d:(bidx + 1) * d].astype(F32)
                merged = gate * y[name] if merged is None else merged + gate * y[name]
            y["merged"] = merged.astype(BF16)

        def residual(c0, width):
            def step():
                out_ref[rs, c0:c0 + width] = h_ref[rs, c0:c0 + width] + _dot(y["merged"], wo_ref[:, c0:c0 + width])
            return step

        steps = [att_out, conv_act, conv_out]
        steps += [pool_group(g, w) for g, w in enumerate(POOL_WINDOWS)]
        steps += [pool_out, merge]
        steps += [residual(c0, 256) for c0 in range(0, d, 256)]
        return steps

    for k in range(ROW_SLABS):
        for c0 in range(0, cw, LANES):
            _conv_chunk(hbuf_ref, dw_ref, conv_ref, k * rows, c0, rows)
        for step in slab_steps(k):
            step()

    hbuf_ref[0:CONV_HALO, :] = hbuf_ref[ts:ts + CONV_HALO, :]
    pbuf_ref[0:POOL_HALO, :] = pbuf_ref[ts:ts + POOL_HALO, :]


def _mixer_tail(h, hc, p, gt, o, weights, ts):
    b, s, d = h.shape
    cw = hc.shape[2]
    pw = p.shape[2]

    def tile(x):
        return pl.BlockSpec((None, ts, x.shape[2]), lambda bi, si: (bi, si, 0))

    return pl.pallas_call(
        _tail_kernel,
        out_shape=jax.ShapeDtypeStruct((b, s, d), F32),
        grid=(b, s // ts),
        in_specs=[tile(h), tile(hc), tile(p), tile(gt), tile(o)] + [_const_spec(w.shape) for w in weights],
        out_specs=tile(h),
        scratch_shapes=[pltpu.VMEM((CONV_HALO + ts, cw), F32),
                        pltpu.VMEM((POOL_HALO + ts, pw), F32),
                        pltpu.VMEM((ts, cw), F32)],
        compiler_params=pltpu.CompilerParams(dimension_semantics=("arbitrary", "arbitrary"),
                                             vmem_limit_bytes=VMEM_LIMIT),
        name="mixer_tail",
    )(h, hc, p, gt, o, *weights)


def _mlp_kernel(x_ref, g_ref, w1_ref, w2_ref, out_ref):
    dff = w1_ref.shape[1]
    rows_per = x_ref.shape[0] // ROW_SLABS
    fc = 1024

    def normed(r0):
        x = x_ref[r0:r0 + rows_per, :]
        ms = jnp.mean(x * x, axis=-1, keepdims=True)
        return ((x * lax.rsqrt(ms + EPS)) * g_ref[...]).astype(BF16)

    xn = normed(0)
    for sidx in range(ROW_SLABS):
        r0 = sidx * rows_per
        acc = x_ref[r0:r0 + rows_per, :]
        xn_next = None
        for ci, c in enumerate(range(0, dff, fc)):
            ff = jnp.maximum(_dot(xn, w1_ref[:, c:c + fc]), 0.0)
            acc = acc + _dot((ff * ff).astype(BF16), w2_ref[c:c + fc, :])
            if ci == (dff // fc) // 2 and sidx + 1 < ROW_SLABS:
                xn_next = normed(r0 + rows_per)
        out_ref[r0:r0 + rows_per, :] = acc
        xn = xn_next


def _mlp(h, g, w1, w2, tm):
    t, d = h.shape
    return pl.pallas_call(
        _mlp_kernel,
        out_shape=jax.ShapeDtypeStruct((t, d), F32),
        grid=(t // tm,),
        in_specs=[pl.BlockSpec((tm, d), lambda i: (i, 0)),
                  _const_spec(g.shape), _const_spec(w1.shape), _const_spec(w2.shape)],
        out_specs=pl.BlockSpec((tm, d), lambda i: (i, 0)),
        compiler_params=pltpu.CompilerParams(dimension_semantics=("arbitrary",),
                                             vmem_limit_bytes=VMEM_LIMIT),
        name="relu2_mlp",
    )(h, g, w1, w2)


def _cumsum_matrix():
    key = (jnp.arange(2 * ATT_BLK) % ATT_BLK)[:, None]
    col = jnp.arange(2 * ATT_BLK)[None, :]
    return jnp.where(col < ATT_BLK, key > col, True).astype(BF16)


def _pick_tile(n, target):
    t = min(n, target)
    while n % t:
        t //= 2
    return t


def kernel(x, mix_norm_g, w_in, gate_b, conv_dw, conv_dw_b, conv_ln_g, conv_ln_b, w_conv_out,
           q_norm_g, k_norm_g, w_att_out, pool_w, pool_scale, w_pool_out, w_o,
           mlp_norm_g, w_mlp_in, w_mlp_out):
    b, s, d = x.shape
    depth = w_in.shape[0]
    aw = w_att_out.shape[1]
    cw = w_conv_out.shape[1]
    pw = w_pool_out.shape[1]
    widths = (aw, cw, pw, gate_b.shape[1] * d)
    assert s % ATT_BLK == 0 and aw % LANES == 0 and LANES == 2 * HEAD_DIM
    tm = _pick_tile(b * s, TOKEN_TILE)
    ts = _pick_tile(s, TOKEN_TILE)
    assert tm % (ROW_SLABS * SUBLANES) == 0 and ts % (ROW_SLABS * SUBLANES) == 0
    tq = _pick_tile(s, ATT_STEP_BLOCKS * ATT_BLK)

    lane = jnp.arange(aw)
    gsum = (lane[:, None] // HEAD_DIM == lane[None, :] // HEAD_DIM).astype(BF16)
    u2 = _cumsum_matrix()

    h = x
    for l in range(depth):
        qg = jnp.tile(q_norm_g[l], aw // HEAD_DIM)[None, :]
        kg = jnp.tile(k_norm_g[l], aw // HEAD_DIM)[None, :]
        q, k, v, hc, p, gt = _inproj(h.reshape(b * s, d), mix_norm_g[l][None, :], w_in[l].astype(BF16),
                                     qg, kg, gsum, gate_b[l].reshape(1, -1), widths, tm)
        o = _attention(q.reshape(b, s, aw), k.reshape(b, s, aw), v.reshape(b, s, aw), u2, tq)
        weights = (conv_dw[l], conv_dw_b[l][None, :], conv_ln_g[l][None, :], conv_ln_b[l][None, :],
                   w_conv_out[l].astype(BF16), pool_w[l].astype(BF16), pool_scale[l][None, :],
                   w_pool_out[l].astype(BF16), w_att_out[l].astype(BF16), w_o[l].astype(BF16))
        h = _mixer_tail(h, hc.reshape(b, s, cw), p.reshape(b, s, pw), gt.reshape(b, s, -1), o,
                        weights, ts)
        h = _mlp(h.reshape(b * s, d), mlp_norm_g[l][None, :], w_mlp_in[l].astype(BF16),
                 w_mlp_out[l].astype(BF16), tm).reshape(b, s, d)
    return h
```

```python
import math

import jax
import jax.numpy as jnp
from jax import lax
from jax.experimental import pallas as pl
from jax.experimental.pallas import tpu as pltpu

F32 = jnp.float32
BF16 = jnp.bfloat16

EPS = 1e-6
CONV_K = 31
HEAD_DIM = 64
LANES = 128
SUBLANES = 8
POOL_WINDOWS = (2, 4, 8, 16)
CONV_HALO = 32
POOL_HALO = 16
ATT_BLK = LANES
ATT_WINDOW = 3
ATT_STEP_BLOCKS = 16
TOKEN_TILE = 1024
ROW_SLABS = 4
LOG2E = math.log2(math.e)
STICK_EXHAUSTED_LOG2 = -105.0 * LOG2E
VMEM_LIMIT = 56 * 1024 * 1024


def _dot(a, b):
    return jnp.dot(a, b, preferred_element_type=F32)


def _tree_max(xs):
    while len(xs) > 1:
        xs = [jnp.maximum(xs[i], xs[i + 1]) if i + 1 < len(xs) else xs[i] for i in range(0, len(xs), 2)]
    return xs[0]


def _const_spec(shape):
    nd = len(shape)
    return pl.BlockSpec(shape, lambda *_: (0,) * nd, pipeline_mode=pl.Buffered(1))


def _inproj_kernel(x_ref, g_ref, w_ref, qg_ref, kg_ref, gsum_ref, gb_ref,
                   q_ref, k_ref, v_ref, hc_ref, p_ref, gt_ref):
    aw = q_ref.shape[1]
    cw = hc_ref.shape[1]
    pw = p_ref.shape[1]
    off_c = 3 * aw
    off_p = off_c + 2 * cw
    off_g = off_p + pw
    rows_per = x_ref.shape[0] // ROW_SLABS

    def normed(r0):
        x = x_ref[r0:r0 + rows_per, :]
        ms = jnp.mean(x * x, axis=-1, keepdims=True)
        return ((x * lax.rsqrt(ms + EPS)) * g_ref[...]).astype(BF16)

    def head_norm(t, gain):
        ss = _dot((t * t).astype(BF16), gsum_ref[...])
        return (t * lax.rsqrt(ss * (1.0 / HEAD_DIM) + EPS)) * gain

    def stages(r0):
        rs = slice(r0, r0 + rows_per)

        def emit_q(t):
            q_ref[rs, :] = (head_norm(t, qg_ref[...]) * (HEAD_DIM ** -0.5 * LOG2E)).astype(BF16)

        def emit_k(t):
            k_ref[rs, :] = head_norm(t, kg_ref[...]).astype(BF16)

        def emit_v(t):
            v_ref[rs, :] = t.astype(BF16)

        def emit_conv(t):
            hc_ref[rs, :] = (t[:, :cw] * jax.nn.sigmoid(t[:, cw:])).astype(BF16)

        def emit_p(t):
            p_ref[rs, :] = t.astype(BF16)

        def emit_gate(c):
            def emit(t):
                gt_ref[rs, c:c + 512] = jax.nn.sigmoid(t + gb_ref[:, c:c + 512]).astype(BF16)
            return emit

        out = [(0, aw, emit_q), (aw, aw, emit_k), (2 * aw, aw, emit_v), (off_c, 2 * cw, emit_conv),
               (off_p, pw, emit_p)]
        out += [(off_g + c, 512, emit_gate(c)) for c in range(0, gt_ref.shape[1], 512)]
        return out

    xn = normed(0)
    for sidx in range(ROW_SLABS):
        xn_next = None
        todo = stages(sidx * rows_per)
        for c, (start, width, emit) in enumerate(todo):
            emit(_dot(xn, w_ref[:, start:start + width]))
            if c == len(todo) // 2 and sidx + 1 < ROW_SLABS:
                xn_next = normed((sidx + 1) * rows_per)
        xn = xn_next


def _inproj(h, g, w, qg, kg, gsum, gate_b, widths, tm):
    t, d = h.shape
    aw, cw, pw, gw = widths
    outs = [(t, aw), (t, aw), (t, aw), (t, cw), (t, pw), (t, gw)]
    consts = (g, w, qg, kg, gsum, gate_b)
    return pl.pallas_call(
        _inproj_kernel,
        out_shape=[jax.ShapeDtypeStruct(s, BF16) for s in outs],
        grid=(t // tm,),
        in_specs=[pl.BlockSpec((tm, d), lambda i: (i, 0))] + [_const_spec(c.shape) for c in consts],
        out_specs=[pl.BlockSpec((tm, s[1]), lambda i: (i, 0)) for s in outs],
        compiler_params=pltpu.CompilerParams(dimension_semantics=("arbitrary",),
                                             vmem_limit_bytes=VMEM_LIMIT),
        name="inproj",
    )(h, *consts)


def _attn_kernel(q_ref, k_ref, v_ref, u2_ref, o_ref, carry_ref, acc_ref):
    qi = pl.program_id(2)
    blk = ATT_BLK
    nq = q_ref.shape[0] // blk
    head0 = lax.broadcasted_iota(jnp.int32, (blk, LANES), 1) < HEAD_DIM
    causal = (lax.broadcasted_iota(jnp.int32, (blk, 2 * blk), 1) & (blk - 1)
              < lax.broadcasted_iota(jnp.int32, (blk, 2 * blk), 0))

    def head_stack(ref, j):
        x = ref[pl.ds(pl.multiple_of(j * blk, blk), blk), :]
        zero = jnp.zeros_like(x)
        return jnp.concatenate([jnp.where(head0, x, zero), jnp.where(head0, zero, x)], axis=0)

    def logits(q_rows, k_stack):
        return lax.dot_general(q_rows, k_stack, (((1,), (1,)), ((), ())),
                               preferred_element_type=F32)

    def log_terms(z, diag):
        neg_abs = pltpu.bitcast(pltpu.bitcast(z, jnp.uint32) | jnp.uint32(0x80000000), F32)
        lb = jnp.minimum(z, 0.0) - jnp.log2(1.0 + jnp.exp2(neg_abs))
        m = lb - z
        if diag:
            m = jnp.where(causal, m, 0.0)
        return lb, m

    def block_cumsums(ms):
        cs = _dot(jnp.concatenate([m.astype(BF16) for m in ms], axis=0), u2_ref[...])
        return [(cs[p * blk:(p + 1) * blk],
                 (jnp.sum(m[:, :blk], axis=1, keepdims=True), jnp.sum(m[:, blk:], axis=1, keepdims=True)))
                for p, m in enumerate(ms)]

    def add_heads(x, cols):
        return jnp.concatenate([x[:, :blk] + cols[0], x[:, blk:] + cols[1]], axis=1)

    def weights(lb, excl, diag):
        a = jnp.exp2(lb + excl)
        if diag:
            a = jnp.where(causal, a, 0.0)
        return a.astype(BF16)

    def stick_max(carries):
        rows = _tree_max(list(carries))
        return jnp.max(_tree_max([rows[r:r + SUBLANES] for r in range(0, blk, SUBLANES)]))

    def sweep(first_step):
        plan = [(n, range(min(ATT_WINDOW, n + 1) if first_step else ATT_WINDOW)) for n in range(nq)]
        rel = sorted({n - back for n, backs in plan for back in backs})
        kst = {r: head_stack(k_ref, qi * nq + r) for r in rel}
        vst = {r: head_stack(v_ref, qi * nq + r) for r in rel}
        zs = [[logits(q_ref[n * blk:(n + 1) * blk, :], kst[n - back]) for back in backs] for n, backs in plan]
        terms = [[log_terms(z, back == 0) for z, back in zip(zrow, backs)] for zrow, (n, backs) in zip(zs, plan)]
        sums = [block_cumsums([m for _, m in trow]) for trow in terms]
        probs, runs = [], []
        for trow, srow, (n, backs) in zip(terms, sums, plan):
            run, prow = None, []
            for back, (lb, _), (excl, tot) in zip(backs, trow, srow):
                if run is not None:
                    excl = add_heads(excl, run)
                run = tot if run is None else (run[0] + tot[0], run[1] + tot[1])
                prow.append(weights(lb, excl, back == 0))
            probs.append(prow)
            runs.append(run)
        for prow, run, (n, backs) in zip(probs, runs, plan):
            acc_ref[n] = _dot(jnp.concatenate(prow, axis=1),
                              jnp.concatenate([vst[n - back] for back in backs], axis=0))
            carry_ref[n] = add_heads(jnp.zeros((blk, 2 * blk), F32), run)

    @pl.when(qi > 0)
    def _():
        sweep(False)

    @pl.when(qi == 0)
    def _():
        sweep(True)

    @pl.when(stick_max([carry_ref[n] for n in range(nq)]) >= STICK_EXHAUSTED_LOG2)
    def _():
        exhausted = [(stick_max([carry_ref[n]]) < STICK_EXHAUSTED_LOG2).astype(jnp.int32) for n in range(nq)]
        for n in range(nq):
            def cond(state):
                j, done = state
                return jnp.logical_and(j >= 0, done == 0)

            def body(state, n=n):
                j, _ = state
                lb, m = log_terms(logits(q_ref[n * blk:(n + 1) * blk, :], head_stack(k_ref, j)), False)
                (excl, tot), = block_cumsums([m])
                carry = carry_ref[n]
                acc_ref[n] += _dot(weights(lb, excl + carry, False), head_stack(v_ref, j))
                carry = add_heads(carry, tot)
                carry_ref[n] = carry
                return j - 1, (stick_max([carry]) < STICK_EXHAUSTED_LOG2).astype(jnp.int32)

            lax.while_loop(cond, body, (qi * nq + n - ATT_WINDOW, exhausted[n]))

    for n in range(nq):
        o_ref[n * blk:(n + 1) * blk, :] = acc_ref[n].astype(o_ref.dtype)


def _attention(q, k, v, u2, tq):
    b, s, aw = q.shape
    groups = aw // LANES
    nq = tq // ATT_BLK
    assert nq >= ATT_WINDOW - 1
    return pl.pallas_call(
        _attn_kernel,
        out_shape=jax.ShapeDtypeStruct((b, s, aw), BF16),
        grid=(b, groups, s // tq),
        in_specs=[pl.BlockSpec((None, tq, LANES), lambda bi, gi, i: (bi, i, gi)),
                  pl.BlockSpec((None, s, LANES), lambda bi, gi, i: (bi, 0, gi)),
                  pl.BlockSpec((None, s, LANES), lambda bi, gi, i: (bi, 0, gi)),
                  _const_spec(u2.shape)],
        out_specs=pl.BlockSpec((None, tq, LANES), lambda bi, gi, i: (bi, i, gi)),
        scratch_shapes=[pltpu.VMEM((nq, ATT_BLK, 2 * ATT_BLK), F32),
                        pltpu.VMEM((nq, ATT_BLK, LANES), F32)],
        compiler_params=pltpu.CompilerParams(
            dimension_semantics=("arbitrary", "arbitrary", "arbitrary"),
            vmem_limit_bytes=VMEM_LIMIT),
        name="stickbreak_attn",
    )(q, k, v, u2)


def _conv_chunk(hbuf_ref, dw_ref, conv_ref, r0, c0, rc):
    acc = None
    for b in range(SUBLANES):
        part = None
        for a in range((CONV_K - 1 - b) // SUBLANES + 1):
            tap = CONV_K - 1 - (SUBLANES * a + b)
            start = CONV_HALO - SUBLANES + r0 - SUBLANES * a
            term = dw_ref[tap:tap + 1, c0:c0 + LANES] * hbuf_ref[start:start + rc + SUBLANES, c0:c0 + LANES]
            part = term if part is None else part + term
        if b:
            part = pltpu.roll(part, b, axis=0)
        part = part[SUBLANES:, :]
        acc = part if acc is None else acc + part
    conv_ref[r0:r0 + rc, c0:c0 + LANES] = acc


def _tail_kernel(h_ref, hc_ref, p_ref, gt_ref, o_ref,
                 dw_ref, dwb_ref, lng_ref, lnb_ref, wconv_ref,
                 poolw_ref, pscale_ref, wpool_ref, watt_ref, wo_ref,
                 out_ref, hbuf_ref, pbuf_ref, conv_ref):
    si = pl.program_id(1)
    ts, d = h_ref.shape
    cw = hbuf_ref.shape[1]
    pw = pbuf_ref.shape[1]
    rows = ts // ROW_SLABS

    @pl.when(si == 0)
    def _():
        hbuf_ref[0:CONV_HALO, :] = jnp.zeros((CONV_HALO, cw), F32)
        pbuf_ref[0:POOL_HALO, :] = jnp.zeros((POOL_HALO, pw), F32)

    hbuf_ref[CONV_HALO:CONV_HALO + ts, :] = hc_ref[...].astype(F32)
    pbuf_ref[POOL_HALO:POOL_HALO + ts, :] = p_ref[...].astype(F32)

    def slab_steps(k):
        rs = slice(k * rows, (k + 1) * rows)
        y = {"pooled": []}

        def conv_act():
            conv = conv_ref[rs, :] + dwb_ref[...]
            mu = jnp.mean(conv, axis=-1, keepdims=True)
            xc = conv - mu
            ln = (xc * lax.rsqrt(jnp.mean(xc * xc, axis=-1, keepdims=True) + EPS)) * lng_ref[...] + lnb_ref[...]
            y["act"] = (ln * jax.nn.sigmoid(ln)).astype(BF16)

        def conv_out():
            y["conv"] = _dot(y["act"], wconv_ref[...])

        def pool_group(g, w):
            def step():
                c0 = g * LANES
                tpos = si * ts + k * rows + lax.broadcasted_iota(jnp.int32, (rows, LANES), 0)
                ext = pbuf_ref[k * rows:k * rows + POOL_HALO + rows, c0:c0 + LANES]
                tot, span = ext, 1
                while span < w:
                    tot = tot + pltpu.roll(tot, span, axis=0)
                    span *= 2
                count = jnp.minimum(tpos + 1, w).astype(F32)
                yg = (tot[POOL_HALO:, :] / count - ext[POOL_HALO:, :]).astype(BF16)
                y["pooled"].append(_dot(yg, poolw_ref[g]))
            return step

        def pool_out():
            yp = jnp.concatenate(y["pooled"], axis=1) * pscale_ref[...]
            y["pool"] = _dot(yp.astype(BF16), wpool_ref[...])

        def att_out():
            y["att"] = _dot(o_ref[rs, :], watt_ref[...])

        def merge():
            merged = None
            for bidx, name in enumerate(("conv", "att", "pool")):
                gate = gt_ref[rs, bidx * d:(bidx + 1) * d].astype(F32)
                merged = gate * y[name] if merged is None else merged + gate * y[name]
            y["merged"] = merged.astype(BF16)

        def residual(c0, width):
            def step():
                out_ref[rs, c0:c0 + width] = h_ref[rs, c0:c0 + width] + _dot(y["merged"], wo_ref[:, c0:c0 + width])
            return step

        steps = [att_out, conv_act, conv_out]
        steps += [pool_group(g, w) for g, w in enumerate(POOL_WINDOWS)]
        steps += [pool_out, merge]
        steps += [residual(c0, 256) for c0 in range(0, d, 256)]
        return steps

    for k in range(ROW_SLABS):
        for c0 in range(0, cw, LANES):
            _conv_chunk(hbuf_ref, dw_ref, conv_ref, k * rows, c0, rows)
        for step in slab_steps(k):
            step()

    hbuf_ref[0:CONV_HALO, :] = hbuf_ref[ts:ts + CONV_HALO, :]
    pbuf_ref[0:POOL_HALO, :] = pbuf_ref[ts:ts + POOL_HALO, :]


def _mixer_tail(h, hc, p, gt, o, weights, ts):
    b, s, d = h.shape
    cw = hc.shape[2]
    pw = p.shape[2]

    def tile(x):
        return pl.BlockSpec((None, ts, x.shape[2]), lambda bi, si: (bi, si, 0))

    return pl.pallas_call(
        _tail_kernel,
        out_shape=jax.ShapeDtypeStruct((b, s, d), F32),
        grid=(b, s // ts),
        in_specs=[tile(h), tile(hc), tile(p), tile(gt), tile(o)] + [_const_spec(w.shape) for w in weights],
        out_specs=tile(h),
        scratch_shapes=[pltpu.VMEM((CONV_HALO + ts, cw), F32),
                        pltpu.VMEM((POOL_HALO + ts, pw), F32),
                        pltpu.VMEM((ts, cw), F32)],
        compiler_params=pltpu.CompilerParams(dimension_semantics=("arbitrary", "arbitrary"),
                                             vmem_limit_bytes=VMEM_LIMIT),
        name="mixer_tail",
    )(h, hc, p, gt, o, *weights)


def _mlp_kernel(x_ref, g_ref, w1_ref, w2_ref, out_ref):
    dff = w1_ref.shape[1]
    rows_per = x_ref.shape[0] // ROW_SLABS
    fc = 1024

    def normed(r0):
        x = x_ref[r0:r0 + rows_per, :]
        ms = jnp.mean(x * x, axis=-1, keepdims=True)
        return ((x * lax.rsqrt(ms + EPS)) * g_ref[...]).astype(BF16)

    xn = normed(0)
    for sidx in range(ROW_SLABS):
        r0 = sidx * rows_per
        acc = x_ref[r0:r0 + rows_per, :]
        xn_next = None
        for ci, c in enumerate(range(0, dff, fc)):
            ff = jnp.maximum(_dot(xn, w1_ref[:, c:c + fc]), 0.0)
            acc = acc + _dot((ff * ff).astype(BF16), w2_ref[c:c + fc, :])
            if ci == (dff // fc) // 2 and sidx + 1 < ROW_SLABS:
                xn_next = normed(r0 + rows_per)
        out_ref[r0:r0 + rows_per, :] = acc
        xn = xn_next


def _mlp(h, g, w1, w2, tm):
    t, d = h.shape
    return pl.pallas_call(
        _mlp_kernel,
        out_shape=jax.ShapeDtypeStruct((t, d), F32),
        grid=(t // tm,),
        in_specs=[pl.BlockSpec((tm, d), lambda i: (i, 0)),
                  _const_spec(g.shape), _const_spec(w1.shape), _const_spec(w2.shape)],
        out_specs=pl.BlockSpec((tm, d), lambda i: (i, 0)),
        compiler_params=pltpu.CompilerParams(dimension_semantics=("arbitrary",),
                                             vmem_limit_bytes=VMEM_LIMIT),
        name="relu2_mlp",
    )(h, g, w1, w2)


def _cumsum_matrix():
    row = jnp.arange(2 * ATT_BLK)[:, None]
    col = jnp.arange(2 * ATT_BLK)[None, :]
    return ((row // ATT_BLK == col // ATT_BLK) & (row % ATT_BLK > col % ATT_BLK)).astype(BF16)


def _pick_tile(n, target):
    t = min(n, target)
    while n % t:
        t //= 2
    return t


def kernel(x, mix_norm_g, w_in, gate_b, conv_dw, conv_dw_b, conv_ln_g, conv_ln_b, w_conv_out,
           q_norm_g, k_norm_g, w_att_out, pool_w, pool_scale, w_pool_out, w_o,
           mlp_norm_g, w_mlp_in, w_mlp_out):
    b, s, d = x.shape
    depth = w_in.shape[0]
    aw = w_att_out.shape[1]
    cw = w_conv_out.shape[1]
    pw = w_pool_out.shape[1]
    widths = (aw, cw, pw, gate_b.shape[1] * d)
    assert s % ATT_BLK == 0 and aw % LANES == 0 and LANES == 2 * HEAD_DIM
    tm = _pick_tile(b * s, TOKEN_TILE)
    ts = _pick_tile(s, TOKEN_TILE)
    assert tm % (ROW_SLABS * SUBLANES) == 0 and ts % (ROW_SLABS * SUBLANES) == 0
    tq = _pick_tile(s, ATT_STEP_BLOCKS * ATT_BLK)

    lane = jnp.arange(aw)
    gsum = (lane[:, None] // HEAD_DIM == lane[None, :] // HEAD_DIM).astype(BF16)
    u2 = _cumsum_matrix()

    h = x
    for l in range(depth):
        qg = jnp.tile(q_norm_g[l], aw // HEAD_DIM)[None, :]
        kg = jnp.tile(k_norm_g[l], aw // HEAD_DIM)[None, :]
        q, k, v, hc, p, gt = _inproj(h.reshape(b * s, d), mix_norm_g[l][None, :], w_in[l].astype(BF16),
                                     qg, kg, gsum, gate_b[l].reshape(1, -1), widths, tm)
        o = _attention(q.reshape(b, s, aw), k.reshape(b, s, aw), v.reshape(b, s, aw), u2, tq)
        weights = (conv_dw[l], conv_dw_b[l][None, :], conv_ln_g[l][None, :], conv_ln_b[l][None, :],
                   w_conv_out[l].astype(BF16), pool_w[l].astype(BF16), pool_scale[l][None, :],
                   w_pool_out[l].astype(BF16), w_att_out[l].astype(BF16), w_o[l].astype(BF16))
        h = _mixer_tail(h, hc.reshape(b, s, cw), p.reshape(b, s, pw), gt.reshape(b, s, -1), o,
                        weights, ts)
        h = _mlp(h.reshape(b * s, d), mlp_norm_g[l][None, :], w_mlp_in[l].astype(BF16),
                 w_mlp_out[l].astype(BF16), tm).reshape(b, s, d)
    return h
```
